```python
import math
import jax, jax.numpy as jnp
from jax import lax
import numpy as np

D_MODEL = 1024
BATCH = 2
SEQ = 16384
DEPTH = 4

HEAD_DIM = 128
RET_HEADS = D_MODEL // (2 * HEAD_DIM)
RET_DK = HEAD_DIM
RET_DV = HEAD_DIM
HGRN_HEADS = D_MODEL // (2 * HEAD_DIM)
HGRN_DK = HEAD_DIM
HGRN_DV = HEAD_DIM
RET_WIDTH = RET_HEADS * RET_DV
HGRN_WIDTH = HGRN_HEADS * HGRN_DV
MIX_WIDTH = RET_WIDTH + HGRN_WIDTH
RET_QK_W = RET_HEADS * RET_DK
HGRN_QK_W = HGRN_HEADS * HGRN_DK
SPLIT_SIZES = (RET_QK_W, RET_QK_W, RET_WIDTH, RET_WIDTH,
               HGRN_QK_W, HGRN_QK_W, HGRN_WIDTH, HGRN_WIDTH)
IN_COLS = sum(SPLIT_SIZES)
D_FF = 2816
CONV_WIDTH = 3
CHUNK = 64
ROPE_BASE = 10000.0
EPS = 1e-6
N_MOD = 6

kernel_name = 'hymba_style_retnet_hgrn2_convffn_adaln'


def rms_norm(x, w):
    xf = x.astype(jnp.float32)
    y = xf * lax.rsqrt(jnp.mean(xf * xf, axis=-1, keepdims=True) + EPS)
    return (y * w.astype(jnp.float32)).astype(x.dtype)


def head_norm(o, w):
    H, D = o.shape[-2], o.shape[-1]
    y = o * lax.rsqrt(jnp.mean(o * o, axis=-1, keepdims=True) + EPS)
    return y * w.astype(jnp.float32).reshape(H, D)


def rotary(x, cos, sin):
    x1, x2 = jnp.split(x, 2, axis=-1)
    c = cos[None, :, None, :]
    s = sin[None, :, None, :]
    return jnp.concatenate([x1 * c - x2 * s, x1 * s + x2 * c], axis=-1)


def to_chunks(t):
    B, S, H, D = t.shape
    n = S // CHUNK
    return t.astype(jnp.float32).reshape(B, n, CHUNK, H, D).transpose(1, 0, 3, 2, 4)


def from_chunks(t):
    n, B, H, C, D = t.shape
    return t.transpose(1, 0, 3, 2, 4).reshape(B, n * C, H, D)


def retention_chunkwise(q, k, v, log_gamma):
    qc, kc, vc = to_chunks(q), to_chunks(k), to_chunks(v)
    B, H, DK, DV = qc.shape[1], qc.shape[2], qc.shape[4], vc.shape[4]
    pos = jnp.arange(CHUNK, dtype=jnp.float32)
    lg = log_gamma.astype(jnp.float32)[:, None]
    rel = pos[:, None] - pos[None, :]
    decay_mask = jnp.where(rel[None] >= 0, jnp.exp(lg[:, :, None] * jnp.maximum(rel, 0.0)[None]), 0.0)
    q_decay = jnp.exp(lg * (pos + 1.0))[None, :, :, None]
    k_decay = jnp.exp(lg * (CHUNK - 1.0 - pos))[None, :, :, None]
    chunk_decay = jnp.exp(lg[:, 0] * CHUNK)[None, :, None, None]

    def step(state, inp):
        qi, ki, vi = inp
        scores = jnp.einsum('bhid,bhjd->bhij', qi, ki) * decay_mask[None]
        intra = jnp.einsum('bhij,bhjv->bhiv', scores, vi)
        inter = jnp.einsum('bhid,bhdv->bhiv', qi, state) * q_decay
        new_state = state * chunk_decay + jnp.einsum('bhjd,bhjv->bhdv', ki * k_decay, vi)
        return new_state, intra + inter

    init = jnp.zeros((B, H, DK, DV), jnp.float32)
    _, out = lax.scan(step, init, (qc, kc, vc))
    return from_chunks(out)


def hgrn2_chunkwise(q, log_f, k, v):
    qc, lfc, kc, vc = to_chunks(q), to_chunks(log_f), to_chunks(k), to_chunks(v)
    B, H, DK, DV = qc.shape[1], qc.shape[2], qc.shape[4], vc.shape[4]
    causal = jnp.tril(jnp.ones((CHUNK, CHUNK), dtype=bool))[:, :, None]

    def step(state, inp):
        qi, lfi, ki, vi = inp
        b = jnp.cumsum(lfi, axis=-2)
        diff = b[:, :, :, None, :] - b[:, :, None, :, :]
        decay = jnp.where(causal, jnp.exp(jnp.where(causal, diff, 0.0)), 0.0)
        scores = jnp.sum(qi[:, :, :, None, :] * ki[:, :, None, :, :] * decay, axis=-1)
        intra = jnp.einsum('bhij,bhjv->bhiv', scores, vi)
        inter = jnp.einsum('bhid,bhdv->bhiv', qi * jnp.exp(b), state)
        b_last = b[:, :, -1:, :]
        new_state = state * jnp.exp(b_last)[:, :, 0, :, None] + jnp.einsum('bhjd,bhjv->bhdv', ki * jnp.exp(b_last - b), vi)
        return new_state, intra + inter

    init = jnp.zeros((B, H, DK, DV), jnp.float32)
    _, out = lax.scan(step, init, (qc, lfc, kc, vc))
    return from_chunks(out)


def hybrid_mixer(h, w_in, w_out, ret_norm_w, hgrn_norm_w, lb, cos, sin, log_gamma):
    B, S, _ = h.shape
    proj = h @ w_in
    idx = np.cumsum(SPLIT_SIZES)[:-1].tolist()
    rq, rk, rv, rg, hq, hf, hi, hg = jnp.split(proj, idx, axis=-1)

    rq = rotary(rq.reshape(B, S, RET_HEADS, RET_DK).astype(jnp.float32), cos, sin)
    rk = rotary(rk.reshape(B, S, RET_HEADS, RET_DK).astype(jnp.float32), cos, sin) * (RET_DK ** -0.5)
    rv = rv.reshape(B, S, RET_HEADS, RET_DV)
    r_out = retention_chunkwise(rq, rk, rv, log_gamma)
    r_out = head_norm(r_out, ret_norm_w) * jax.nn.silu(rg.astype(jnp.float32).reshape(B, S, RET_HEADS, RET_DV))

    z = hf.astype(jnp.float32).reshape(B, S, HGRN_HEADS, HGRN_DK)
    lbh = lb.astype(jnp.float32).reshape(HGRN_HEADS, HGRN_DK)
    log_f = jax.nn.log_sigmoid(z) + jnp.log1p(lbh * jnp.exp(-z))
    k_in = (1.0 - lbh) * jax.nn.sigmoid(-z)
    q_h = jax.nn.silu(hq.astype(jnp.float32).reshape(B, S, HGRN_HEADS, HGRN_DK))
    v_h = hi.reshape(B, S, HGRN_HEADS, HGRN_DV)
    g_out = hgrn2_chunkwise(q_h, log_f, k_in, v_h)
    g_out = head_norm(g_out, hgrn_norm_w) * jax.nn.silu(hg.astype(jnp.float32).reshape(B, S, HGRN_HEADS, HGRN_DV))

    merged = jnp.concatenate([r_out.reshape(B, S, RET_WIDTH), g_out.reshape(B, S, HGRN_WIDTH)], axis=-1)
    return merged.astype(h.dtype) @ w_out


def conv_ffn(h, w_gate, w_val, conv_w, conv_b, w_down):
    a = h @ w_gate
    S = a.shape[1]
    ap = jnp.pad(a, ((0, 0), (CONV_WIDTH - 1, 0), (0, 0)))
    acc = conv_b[None, None, :] + ap[:, 0:S] * conv_w[0]
    for j in range(1, CONV_WIDTH):
        acc = acc + ap[:, j:j + S] * conv_w[j]
    return (jax.nn.silu(acc) * (h @ w_val)) @ w_down


def setup_inputs(seed: int = 0) -> dict:
    key = jax.random.key(seed)
    ks = jax.random.split(key, 20)
    f32 = jnp.float32
    nrm = lambda k, shape, s: (jax.random.normal(k, shape, f32) * s).astype(f32)
    return {
        'x': nrm(ks[0], (BATCH, SEQ, D_MODEL), 1.0),
        'c': nrm(ks[1], (BATCH, D_MODEL), 1.0),
        'w_in': nrm(ks[2], (DEPTH, D_MODEL, IN_COLS), D_MODEL ** -0.5),
        'w_out': nrm(ks[3], (DEPTH, MIX_WIDTH, D_MODEL), MIX_WIDTH ** -0.5),
        'ret_norm_w': 1.0 + nrm(ks[4], (DEPTH, RET_WIDTH), 0.02),
        'hgrn_norm_w': 1.0 + nrm(ks[5], (DEPTH, HGRN_WIDTH), 0.02),
        'hgrn_lb_logits': nrm(ks[6], (DEPTH, HGRN_HEADS * HGRN_DK), 0.1),
        'norm1_w': 1.0 + nrm(ks[7], (DEPTH, D_MODEL), 0.02),
        'norm2_w': 1.0 + nrm(ks[8], (DEPTH, D_MODEL), 0.02),
        'ada_w': nrm(ks[9], (DEPTH, D_MODEL, N_MOD * D_MODEL), 0.5 * D_MODEL ** -0.5),
        'ada_b': nrm(ks[10], (DEPTH, N_MOD * D_MODEL), 0.01),
        'w_gate': nrm(ks[11], (DEPTH, D_MODEL, D_FF), D_MODEL ** -0.5),
        'w_val': nrm(ks[12], (DEPTH, D_MODEL, D_FF), D_MODEL ** -0.5),
        'conv_w': nrm(ks[13], (DEPTH, CONV_WIDTH, D_FF), CONV_WIDTH ** -0.5),
        'conv_b': nrm(ks[14], (DEPTH, D_FF), 0.01),
        'w_down': nrm(ks[15], (DEPTH, D_FF, D_MODEL), D_FF ** -0.5),
        'final_norm_w': 1.0 + nrm(ks[16], (D_MODEL,), 0.02),
    }


def reference(x, c, w_in, w_out, ret_norm_w, hgrn_norm_w, hgrn_lb_logits, norm1_w, norm2_w,
              ada_w, ada_b, w_gate, w_val, conv_w, conv_b, w_down, final_norm_w):
    S = x.shape[1]
    pos = jnp.arange(S, dtype=jnp.float32)
    inv_freq = ROPE_BASE ** (-jnp.arange(0, RET_DK, 2, dtype=jnp.float32) / RET_DK)
    ang = pos[:, None] * inv_freq[None, :]
    cos, sin = jnp.cos(ang), jnp.sin(ang)
    log_gamma = jnp.log(1.0 - jnp.exp2(-5.0 - jnp.arange(RET_HEADS, dtype=jnp.float32)))
    p = jax.nn.softmax(hgrn_lb_logits.astype(jnp.float32), axis=0)
    lower_bounds = jnp.cumsum(p, axis=0) - p[0:1]
    c_act = jax.nn.silu(c)

    for l in range(DEPTH):
        mod = (c_act @ ada_w[l] + ada_b[l])[:, None, :]
        sh1, sc1, g1, sh2, sc2, g2 = jnp.split(mod, N_MOD, axis=-1)
        h = rms_norm(x, norm1_w[l]) * (1.0 + sc1) + sh1
        x = x + g1 * hybrid_mixer(h, w_in[l], w_out[l], ret_norm_w[l], hgrn_norm_w[l],
                                  lower_bounds[l], cos, sin, log_gamma)
        h = rms_norm(x, norm2_w[l]) * (1.0 + sc2) + sh2
        x = x + g2 * conv_ffn(h, w_gate[l], w_val[l], conv_w[l], conv_b[l], w_down[l])

    return rms_norm(x, final_norm_w)
```

```python
import functools

import numpy as np
import jax
import jax.numpy as jnp
from jax import lax
from jax.experimental import pallas as pl
from jax.experimental.pallas import tpu as pltpu

F32 = jnp.float32
BF16 = jnp.bfloat16

D_MODEL = 1024
HEAD_DIM = 128
N_HEADS = 4
GROUP_W = N_HEADS * HEAD_DIM
IN_COLS = 8 * GROUP_W
D_FF = 2816
CONV_WIDTH = 3
ROPE_BASE = 10000.0
EPS = 1e-6
N_MOD = 6

SEQ_TILE = 512
RET_CHUNK = 256
HG_CHUNK = 64
HG_LEVELS = (32, 16, 8, 4, 2, 1)
FF_BLOCK = 256
CARRY_ROWS = 8
ADA_BLOCK = 1536
VMEM_LIMIT_BYTES = 56 * 1024 * 1024


def _const_spec(shape):
    nd = len(shape)
    return pl.BlockSpec(shape, lambda *_: (0,) * nd, pipeline_mode=pl.Buffered(1))


def _sigmoid(z):
    return 1.0 / (1.0 + jnp.exp(-z))


def _silu(z):
    return z * _sigmoid(z)


def _dot(a, b):
    return jnp.dot(a, b, preferred_element_type=F32)


def _dot_nt(a, b):
    return lax.dot_general(a, b, (((1,), (1,)), ((), ())), preferred_element_type=F32)


def _dot_tn(a, b):
    return lax.dot_general(a, b, (((0,), (0,)), ((), ())), preferred_element_type=F32)


def _split3(a):
    hi = a.astype(BF16)
    r1 = a - hi.astype(F32)
    mid = r1.astype(BF16)
    lo = (r1 - mid.astype(F32)).astype(BF16)
    return hi, mid, lo


def _ada_kernel(c_ref, w_ref, b_ref, o_ref):
    c = c_ref[...]
    o_ref[...] = jnp.dot(_silu(c), w_ref[...], preferred_element_type=F32,
                         precision=lax.Precision.HIGHEST) + b_ref[...]


def _ada_mods(c, ada_w, ada_b):
    depth, d, n = ada_w.shape
    b = c.shape[0]
    rows = 8
    c_pad = jnp.zeros((rows, d), F32).at[:b].set(c)
    out = pl.pallas_call(
        _ada_kernel,
        grid=(depth, n // ADA_BLOCK),
        in_specs=[
            pl.BlockSpec((rows, d), lambda l, j: (0, 0)),
            pl.BlockSpec((None, d, ADA_BLOCK), lambda l, j: (l, 0, j)),
            pl.BlockSpec((None, 1, ADA_BLOCK), lambda l, j: (l, 0, j)),
        ],
        out_specs=pl.BlockSpec((None, rows, ADA_BLOCK), lambda l, j: (l, 0, j)),
        out_shape=jax.ShapeDtypeStruct((depth, rows, n), F32),
        name="ada_mod",
    )(c_pad, ada_w, ada_b.reshape(depth, 1, n))
    return out[:, :b].reshape(depth, b, N_MOD, d)


def _rope_kernel(f_ref, cos_ref, sin_ref):
    rows = cos_ref.shape[0]
    pos = (lax.broadcasted_iota(jnp.int32, (rows, HEAD_DIM), 0) + pl.program_id(0) * rows).astype(F32)
    ang = pos * f_ref[...]
    lane = lax.broadcasted_iota(jnp.int32, (rows, HEAD_DIM), 1)
    cos_ref[...] = jnp.cos(ang)
    sin_ref[...] = jnp.where(lane < HEAD_DIM // 2, -jnp.sin(ang), jnp.sin(ang))


def _rope_tables(seq):
    inv_freq = ROPE_BASE ** (-jnp.arange(0, HEAD_DIM, 2, dtype=F32) / HEAD_DIM)
    f2 = jnp.concatenate([inv_freq, inv_freq]).reshape(1, HEAD_DIM)
    rows = 2048
    return pl.pallas_call(
        _rope_kernel,
        grid=(seq // rows,),
        in_specs=[pl.BlockSpec((1, HEAD_DIM), lambda i: (0, 0))],
        out_specs=[pl.BlockSpec((rows, HEAD_DIM), lambda i: (i, 0))] * 2,
        out_shape=[jax.ShapeDtypeStruct((seq, HEAD_DIM), F32)] * 2,
        name="rope_tables",
    )(f2)


def _hg_level_masks():
    c = HG_CHUNK
    i = np.arange(c)[:, None]
    j = np.arange(c)[None, :]
    out = []
    for m in HG_LEVELS:
        blk = 2 * m
        out.append(((i // blk == j // blk) & (i % blk >= m) & (j % blk < m)).astype(np.float32))
    return np.stack(out)


def _rms_mod(x, w, scale, shift):
    y = x * lax.rsqrt(jnp.mean(x * x, axis=-1, keepdims=True) + EPS)
    return y * w * (1.0 + scale) + shift


def _head_norm_gate(o, w, g):
    y = o * lax.rsqrt(jnp.mean(o * o, axis=-1, keepdims=True) + EPS)
    return y * w * _silu(g)


def _level_reference(b, m):
    c = b.shape[0]
    blk = 2 * m
    if blk >= 8:
        b3 = b.reshape(c // blk, blk, HEAD_DIM)
        return jnp.broadcast_to(b3[:, m - 1:m, :], b3.shape).reshape(c, HEAD_DIM)
    pos = lax.broadcasted_iota(jnp.int32, b.shape, 0) % blk
    ref = b
    for p in range(blk):
        shift = p - (m - 1)
        if shift == 0:
            continue
        ref = jnp.where(pos == p, pltpu.roll(b, shift % c, 0), ref)
    return ref


def _mixer_kernel(x_ref, mod_ref, n1_ref, win_ref, wout_ref, rnw_ref, hnw_ref, lb_ref,
                  cos_ref, sin_ref, rdm_ref, rqd_ref, rkd_ref, rcd_ref, tri_ref, lmask_ref,
                  o_ref, proj_ref, merged_ref, rstate_ref, hstate_ref):
    ts = x_ref.shape[0]

    @pl.when(pl.program_id(1) == 0)
    def _():
        rstate_ref[...] = jnp.zeros_like(rstate_ref)
        hstate_ref[...] = jnp.zeros_like(hstate_ref)

    x = x_ref[...]
    mod = mod_ref[...]
    hb = _rms_mod(x, n1_ref[...], mod[1:2], mod[0:1]).astype(BF16)
    for n in range(IN_COLS // GROUP_W):
        cols = slice(n * GROUP_W, (n + 1) * GROUP_W)
        proj_ref[:, cols] = _dot(hb, win_ref[:, cols])

    for c in range(ts // RET_CHUNK):
        rows = slice(c * RET_CHUNK, (c + 1) * RET_CHUNK)
        cs = cos_ref[rows, :]
        sn = sin_ref[rows, :]
        for hd in range(N_HEADS):
            def col(group):
                lo = group * GROUP_W + hd * HEAD_DIM
                return slice(lo, lo + HEAD_DIM)
            q = proj_ref[rows, col(0)]
            k = proj_ref[rows, col(1)]
            v = proj_ref[rows, col(2)].astype(BF16)
            g = proj_ref[rows, col(3)]
            q = q * cs + pltpu.roll(q, HEAD_DIM // 2, 1) * sn
            k = (k * cs + pltpu.roll(k, HEAD_DIM // 2, 1) * sn) * (HEAD_DIM ** -0.5)
            qb = q.astype(BF16)
            scores = _dot_nt(qb, k.astype(BF16)) * rdm_ref[hd]
            state = rstate_ref[hd]
            o = _dot(scores.astype(BF16), v) + _dot(qb, state.astype(BF16)) * rqd_ref[hd]
            rstate_ref[hd] = state * rcd_ref[hd] + _dot_tn((k * rkd_ref[hd]).astype(BF16), v)
            merged_ref[rows, hd * HEAD_DIM:(hd + 1) * HEAD_DIM] = _head_norm_gate(
                o, rnw_ref[:, hd * HEAD_DIM:(hd + 1) * HEAD_DIM], g).astype(BF16)

    lb = lb_ref[...]
    tri = tri_ref[...]

    def hg_chunk(c, carry):
        r0 = pl.multiple_of(c * HG_CHUNK, HG_CHUNK)
        rows = pl.ds(r0, HG_CHUNK)
        for hd in range(N_HEADS):
            def col(group):
                lo = (4 + group) * GROUP_W + hd * HEAD_DIM
                return slice(lo, lo + HEAD_DIM)
            hcols = slice(hd * HEAD_DIM, (hd + 1) * HEAD_DIM)
            q = _silu(proj_ref[rows, col(0)])
            z = proj_ref[rows, col(1)]
            v = proj_ref[rows, col(2)]
            g = proj_ref[rows, col(3)]
            vb = v.astype(BF16)
            lbh = lb[:, hcols]
            e = jnp.exp(-jnp.abs(z))
            r = 1.0 / (1.0 + e)
            er = e * r
            sig_p = jnp.where(z >= 0, r, er)
            sig_n = jnp.where(z >= 0, er, r)
            a = jnp.log(lbh + (1.0 - lbh) * sig_p)
            k = (1.0 - lbh) * sig_n
            a_hi, a_mid, a_lo = _split3(a)
            b = _dot(tri, a_hi) + _dot(tri, a_mid) + _dot(tri, a_lo)
            pos = lax.broadcasted_iota(jnp.int32, b.shape, 0)
            s = jnp.zeros((HG_CHUNK, HG_CHUNK), F32)
            for li, m in enumerate(HG_LEVELS):
                dec = jnp.exp(-jnp.abs(b - _level_reference(b, m)))
                xl = (jnp.where(pos % (2 * m) >= m, q, k) * dec).astype(BF16)
                s = s + _dot_nt(xl, xl) * lmask_ref[li]
            diag = jnp.sum(q * k, axis=-1, keepdims=True)
            state_t = hstate_ref[hd]
            o = _dot(s.astype(BF16), vb) + diag * v
            o = o + _dot_nt((q * jnp.exp(b)).astype(BF16), state_t.astype(BF16))
            b_last = b[HG_CHUNK - 1:HG_CHUNK, :]
            kd = (k * jnp.exp(b_last - b)).astype(BF16)
            hstate_ref[hd] = state_t * jnp.exp(b_last) + _dot_tn(vb, kd)
            merged_ref[rows, GROUP_W + hd * HEAD_DIM:GROUP_W + (hd + 1) * HEAD_DIM] = _head_norm_gate(
                o, hnw_ref[:, hcols], g).astype(BF16)
        return carry

    lax.fori_loop(0, ts // HG_CHUNK, hg_chunk, 0)

    o_ref[...] = x + mod[2:3] * _dot(merged_ref[...], wout_ref[...])


def _mixer_layer(x, mod_l, n1, win, wout, rnw, hnw, lb, cos2, sin2, rdm, rqd, rkd, rcd, tri, lmask):
    bsz, seq, d = x.shape
    ts = SEQ_TILE
    grid = (bsz, seq // ts)
    x_spec = pl.BlockSpec((None, ts, d), lambda b, t: (b, t, 0))
    tab_spec = pl.BlockSpec((ts, HEAD_DIM), lambda b, t: (t, 0))
    return pl.pallas_call(
        _mixer_kernel,
        grid=grid,
        in_specs=[
            x_spec,
            pl.BlockSpec((None, N_MOD, d), lambda b, t: (b, 0, 0)),
            _const_spec(n1.shape), _const_spec(win.shape), _const_spec(wout.shape),
            _const_spec(rnw.shape), _const_spec(hnw.shape), _const_spec(lb.shape),
            tab_spec, tab_spec,
            _const_spec(rdm.shape), _const_spec(rqd.shape), _const_spec(rkd.shape), _const_spec(rcd.shape),
            _const_spec(tri.shape), _const_spec(lmask.shape),
        ],
        out_specs=x_spec,
        out_shape=jax.ShapeDtypeStruct(x.shape, x.dtype),
        scratch_shapes=[
            pltpu.VMEM((ts, IN_COLS), F32),
            pltpu.VMEM((ts, 2 * GROUP_W), BF16),
            pltpu.VMEM((N_HEADS, HEAD_DIM, HEAD_DIM), F32),
            pltpu.VMEM((N_HEADS, HEAD_DIM, HEAD_DIM), F32),
        ],
        compiler_params=pltpu.CompilerParams(
            dimension_semantics=("arbitrary", "arbitrary"), vmem_limit_bytes=VMEM_LIMIT_BYTES),
        name="mixer",
    )(x, mod_l, n1, win, wout, rnw, hnw, lb, cos2, sin2, rdm, rqd, rkd, rcd, tri, lmask)


def _ffn_kernel(*refs, final):
    if final:
        (x_ref, mod_ref, n2_ref, wg_ref, wv_ref, wd_ref, cw_ref, cb_ref, fw_ref,
         o_ref, a_ref, y_ref) = refs
    else:
        (x_ref, mod_ref, n2_ref, wg_ref, wv_ref, wd_ref, cw_ref, cb_ref,
         o_ref, a_ref, y_ref) = refs
    ts = x_ref.shape[0]

    @pl.when(pl.program_id(1) == 0)
    def _():
        a_ref[0:CARRY_ROWS, :] = jnp.zeros((CARRY_ROWS, D_FF), F32)

    @pl.when(pl.program_id(1) != 0)
    def _():
        a_ref[0:CARRY_ROWS, :] = a_ref[ts:ts + CARRY_ROWS, :]

    x = x_ref[...]
    mod = mod_ref[...]
    hb = _rms_mod(x, n2_ref[...], mod[4:5], mod[3:4]).astype(BF16)
    for j in range(D_FF // FF_BLOCK):
        cols = slice(j * FF_BLOCK, (j + 1) * FF_BLOCK)
        a_ref[CARRY_ROWS:CARRY_ROWS + ts, cols] = _dot(hb, wg_ref[:, cols])
        acc = cb_ref[:, cols] + a_ref[CARRY_ROWS - 2:CARRY_ROWS - 2 + ts, cols] * cw_ref[0:1, cols]
        acc = acc + a_ref[CARRY_ROWS - 1:CARRY_ROWS - 1 + ts, cols] * cw_ref[1:2, cols]
        acc = acc + a_ref[CARRY_ROWS:CARRY_ROWS + ts, cols] * cw_ref[2:3, cols]
        y_ref[:, cols] = (_silu(acc) * _dot(hb, wv_ref[:, cols])).astype(BF16)
    out = x + mod[5:6] * _dot(y_ref[...], wd_ref[...])
    if final:
        out = out * lax.rsqrt(jnp.mean(out * out, axis=-1, keepdims=True) + EPS) * fw_ref[...]
    o_ref[...] = out


def _ffn_layer(x, mod_l, n2, wg, wv, wd, cw, cb, final_w=None):
    bsz, seq, d = x.shape
    ts = SEQ_TILE
    final = final_w is not None
    x_spec = pl.BlockSpec((None, ts, d), lambda b, t: (b, t, 0))
    args = [x, mod_l, n2, wg, wv, wd, cw, cb] + ([final_w] if final else [])
    in_specs = [x_spec, pl.BlockSpec((None, N_MOD, d), lambda b, t: (b, 0, 0))]
    in_specs += [_const_spec(a.shape) for a in args[2:]]
    return pl.pallas_call(
        functools.partial(_ffn_kernel, final=final),
        grid=(bsz, seq // ts),
        in_specs=in_specs,
        out_specs=x_spec,
        out_shape=jax.ShapeDtypeStruct(x.shape, x.dtype),
        scratch_shapes=[
            pltpu.VMEM((ts + 2 * CARRY_ROWS, D_FF), F32),
            pltpu.VMEM((ts, D_FF), BF16),
        ],
        compiler_params=pltpu.CompilerParams(
            dimension_semantics=("arbitrary", "arbitrary"), vmem_limit_bytes=VMEM_LIMIT_BYTES),
        name="ffn",
    )(*args)


def _retention_tables():
    log_gamma = jnp.log(1.0 - jnp.exp2(-5.0 - jnp.arange(N_HEADS, dtype=F32)))
    lg = log_gamma[:, None, None]
    pos = jnp.arange(RET_CHUNK, dtype=F32)
    rel = pos[:, None] - pos[None, :]
    decay_mask = jnp.where(rel[None] >= 0, jnp.exp(lg * jnp.maximum(rel, 0.0)[None]), 0.0)
    ones = jnp.ones((1, 1, HEAD_DIM), F32)
    q_decay = jnp.exp(lg * (pos + 1.0)[None, :, None]) * ones
    k_decay = jnp.exp(lg * (RET_CHUNK - 1.0 - pos)[None, :, None]) * ones
    chunk_decay = jnp.exp(lg * RET_CHUNK) * jnp.ones((1, HEAD_DIM, HEAD_DIM), F32)
    return decay_mask, q_decay, k_decay, chunk_decay


def kernel(x, c, w_in, w_out, ret_norm_w, hgrn_norm_w, hgrn_lb_logits, norm1_w, norm2_w,
           ada_w, ada_b, w_gate, w_val, conv_w, conv_b, w_down, final_norm_w):
    depth = w_in.shape[0]
    seq = x.shape[1]
    assert x.shape[2] == D_MODEL and seq % SEQ_TILE == 0 and SEQ_TILE % RET_CHUNK == 0

    mods = _ada_mods(c, ada_w, ada_b)
    cos2, sin2 = _rope_tables(seq)
    rdm, rqd, rkd, rcd = _retention_tables()
    tri = jnp.asarray(np.tril(np.ones((HG_CHUNK, HG_CHUNK), np.float32)), BF16)
    lmask = jnp.asarray(_hg_level_masks())
    p = jax.nn.softmax(hgrn_lb_logits.astype(F32), axis=0)
    lower_bounds = jnp.cumsum(p, axis=0) - p[0:1]

    row = lambda v: v.reshape(1, -1)
    for l in range(depth):
        x = _mixer_layer(x, mods[l], row(norm1_w[l]), w_in[l].astype(BF16), w_out[l].astype(BF16),
                         row(ret_norm_w[l]), row(hgrn_norm_w[l]), row(lower_bounds[l]),
                         cos2, sin2, rdm, rqd, rkd, rcd, tri, lmask)
        x = _ffn_layer(x, mods[l], row(norm2_w[l]), w_gate[l].astype(BF16), w_val[l].astype(BF16),
                       w_down[l].astype(BF16), conv_w[l], row(conv_b[l]),
                       final_w=row(final_norm_w) if l == depth - 1 else None)
    return x
```

```python
import functools

import numpy as np
import jax
import jax.numpy as jnp
from jax import lax
from jax.experimental import pallas as pl
from jax.experimental.pallas import tpu as pltpu

F32 = jnp.float32
BF16 = jnp.bfloat16

SUBLANES = 8
D_MODEL = 1024
HEAD_DIM = 128
N_HEADS = 4
GROUP_W = N_HEADS * HEAD_DIM
IN_COLS = 8 * GROUP_W
D_FF = 2816
CONV_WIDTH = 3
ROPE_BASE = 10000.0
EPS = 1e-6
N_MOD = 6

SEQ_TILE = 512
RET_CHUNK = 256
HG_CHUNK = 128
HG_LEVELS = (64, 32, 16, 8, 4, 2, 1)
HG_CHUNKS = SEQ_TILE // HG_CHUNK
HEADS_PER_SCAN = SUBLANES // HG_CHUNKS
SCAN_PITCH = HG_CHUNK + SUBLANES
FF_BLOCK = 256
CARRY_ROWS = 8
ADA_BLOCK = 1536
VMEM_LIMIT_BYTES = 56 * 1024 * 1024


def _const_spec(shape):
    nd = len(shape)
    return pl.BlockSpec(shape, lambda *_: (0,) * nd, pipeline_mode=pl.Buffered(1))


def _sigmoid(z):
    return 1.0 / (1.0 + jnp.exp(-z))


def _silu(z):
    return z * _sigmoid(z)


def _dot(a, b):
    return jnp.dot(a, b, preferred_element_type=F32)


def _dot_nt(a, b):
    return lax.dot_general(a, b, (((1,), (1,)), ((), ())), preferred_element_type=F32)


def _dot_tn(a, b):
    return lax.dot_general(a, b, (((0,), (0,)), ((), ())), preferred_element_type=F32)


def _ada_kernel(c_ref, w_ref, b_ref, o_ref):
    c = c_ref[...]
    o_ref[...] = jnp.dot(_silu(c), w_ref[...], preferred_element_type=F32,
                         precision=lax.Precision.HIGHEST) + b_ref[...]


def _ada_mods(c, ada_w, ada_b):
    depth, d, n = ada_w.shape
    b = c.shape[0]
    rows = SUBLANES
    c_pad = jnp.zeros((rows, d), F32).at[:b].set(c)
    out = pl.pallas_call(
        _ada_kernel,
        grid=(depth, n // ADA_BLOCK),
        in_specs=[
            pl.BlockSpec((rows, d), lambda l, j: (0, 0)),
            pl.BlockSpec((None, d, ADA_BLOCK), lambda l, j: (l, 0, j)),
            pl.BlockSpec((None, 1, ADA_BLOCK), lambda l, j: (l, 0, j)),
        ],
        out_specs=pl.BlockSpec((None, rows, ADA_BLOCK), lambda l, j: (l, 0, j)),
        out_shape=jax.ShapeDtypeStruct((depth, rows, n), F32),
        name="ada_mod",
    )(c_pad, ada_w, ada_b.reshape(depth, 1, n))
    return out[:, :b].reshape(depth, b, N_MOD, d)


def _rope_kernel(f_ref, cos_ref, sin_ref):
    rows = cos_ref.shape[0]
    pos = (lax.broadcasted_iota(jnp.int32, (rows, HEAD_DIM), 0) + pl.program_id(0) * rows).astype(F32)
    ang = pos * f_ref[...]
    lane = lax.broadcasted_iota(jnp.int32, (rows, HEAD_DIM), 1)
    cos_ref[...] = jnp.cos(ang)
    sin_ref[...] = jnp.where(lane < HEAD_DIM // 2, -jnp.sin(ang), jnp.sin(ang))


def _rope_tables(seq):
    inv_freq = ROPE_BASE ** (-jnp.arange(0, HEAD_DIM, 2, dtype=F32) / HEAD_DIM)
    f2 = jnp.concatenate([inv_freq, inv_freq]).reshape(1, HEAD_DIM)
    rows = 2048
    return pl.pallas_call(
        _rope_kernel,
        grid=(seq // rows,),
        in_specs=[pl.BlockSpec((1, HEAD_DIM), lambda i: (0, 0))],
        out_specs=[pl.BlockSpec((rows, HEAD_DIM), lambda i: (i, 0))] * 2,
        out_shape=[jax.ShapeDtypeStruct((seq, HEAD_DIM), F32)] * 2,
        name="rope_tables",
    )(f2)


def _hg_level_masks():
    c = HG_CHUNK
    i = np.arange(c)[:, None]
    j = np.arange(c)[None, :]
    out = []
    for m in HG_LEVELS:
        blk = 2 * m
        out.append(((i // blk == j // blk) & (i % blk >= m) & (j % blk < m)).astype(np.float32))
    return np.stack(out)


def _rms_mod(x, w, scale, shift):
    y = x * lax.rsqrt(jnp.mean(x * x, axis=-1, keepdims=True) + EPS)
    return y * w * (1.0 + scale) + shift


def _head_norm(o, w):
    return o * lax.rsqrt(jnp.mean(o * o, axis=-1, keepdims=True) + EPS) * w


def _head_cols(hd):
    return slice(hd * HEAD_DIM, (hd + 1) * HEAD_DIM)


def _scan_base(hd, c):
    return ((hd % HEADS_PER_SCAN) * HG_CHUNKS + c) * SCAN_PITCH


def _hg_level_factors(q, k, a, b, b_row):
    c = q.shape[0]
    sub = lax.broadcasted_iota(jnp.int32, q.shape, 0) % SUBLANES
    out = []
    for m in HG_LEVELS:
        blk = 2 * m
        if m >= SUBLANES:
            parts = []
            for base in range(0, c, blk):
                ref = b_row(base + m - 1)
                lo, up = slice(base, base + m), slice(base + m, base + blk)
                parts.append(k[lo] * jnp.exp2(ref - b[lo]))
                parts.append(q[up] * jnp.exp2(b[up] - ref))
            x = jnp.concatenate(parts, axis=0)
        elif m == 1:
            x = jnp.where(sub % 2 == 1, q * jnp.exp2(a), k)
        else:
            refs = []
            for base in range(0, c, SUBLANES):
                r = None
                for off in range(m - 1, SUBLANES, blk):
                    row = jnp.broadcast_to(b_row(base + off), (SUBLANES, HEAD_DIM))
                    r = row if r is None else jnp.where(sub[:SUBLANES] < off - m + 1, r, row)
                refs.append(r)
            e = b - jnp.concatenate(refs, axis=0)
            x = jnp.where(sub % blk >= m, q, k) * jnp.exp2(jnp.minimum(e, -e))
        out.append(x.astype(BF16))
    return out


def _mixer_kernel(x_ref, mod_ref, n1_ref, win_ref, wout_ref, rnw_ref, hnw_ref, lb_ref,
                  cos_ref, sin_ref, rdm_ref, rqd_ref, rkd_ref, rcd_ref, lmask_ref,
                  o_ref,
                  rq_ref, rk_ref, rkdec_ref, rv_ref, rg_ref,
                  hq_ref, hk_ref, hv_ref, hvb_ref, hg_ref, a_ref, b_ref,
                  merged_ref, rstate_ref, hstate_ref):
    ts = x_ref.shape[0]

    @pl.when(pl.program_id(1) == 0)
    def _():
        rstate_ref[...] = jnp.zeros_like(rstate_ref)
        hstate_ref[...] = jnp.zeros_like(hstate_ref)

    x = x_ref[...]
    mod = mod_ref[...]
    hb = _rms_mod(x, n1_ref[...], mod[1:2], mod[0:1]).astype(BF16)

    def proj(group):
        return _dot(hb, win_ref[:, group * GROUP_W:(group + 1) * GROUP_W])

    def rotary(t):
        parts = []
        for hd in range(N_HEADS):
            th = t[:, _head_cols(hd)]
            parts.append(th * cos_ref[...] + pltpu.roll(th, HEAD_DIM // 2, 1) * sin_ref[...])
        return jnp.concatenate(parts, axis=1)

    rq_ref[...] = rotary(proj(0)).astype(BF16)
    rk = rotary(proj(1)) * (HEAD_DIM ** -0.5)
    rk_ref[...] = rk.astype(BF16)
    rkdec_ref[...] = (rk * rkd_ref[...]).astype(BF16)
    rv_ref[...] = proj(2).astype(BF16)
    rg_ref[...] = _silu(proj(3))
    hq_ref[...] = _silu(proj(4))
    z = proj(5)
    lb = lb_ref[...]
    e = jnp.exp(-jnp.abs(z))
    r = 1.0 / (1.0 + e)
    er = e * r
    a2 = jnp.log2(lb + (1.0 - lb) * jnp.where(z >= 0, r, er))
    hk_ref[...] = (1.0 - lb) * jnp.where(z >= 0, er, r)
    for hd in range(N_HEADS):
        for c in range(HG_CHUNKS):
            a_ref[hd // HEADS_PER_SCAN, pl.ds(_scan_base(hd, c), HG_CHUNK), :] = (
                a2[c * HG_CHUNK:(c + 1) * HG_CHUNK, _head_cols(hd)])
    hv = proj(6)
    hv_ref[...] = hv
    hvb_ref[...] = hv.astype(BF16)
    hg_ref[...] = _silu(proj(7))

    for p in range(N_HEADS // HEADS_PER_SCAN):
        run = jnp.zeros((SUBLANES, HEAD_DIM), F32)
        for t in range(HG_CHUNK):
            rows = pl.ds(t, SUBLANES, stride=SCAN_PITCH)
            run = run + a_ref[p, rows, :]
            b_ref[p, rows, :] = run

    for c in range(ts // RET_CHUNK):
        rows = slice(c * RET_CHUNK, (c + 1) * RET_CHUNK)
        for hd in range(N_HEADS):
            cols = _head_cols(hd)
            qb = rq_ref[rows, cols]
            vb = rv_ref[rows, cols]
            scores = _dot_nt(qb, rk_ref[rows, cols]) * rdm_ref[hd]
            state = rstate_ref[hd]
            o = _dot(scores.astype(BF16), vb) + _dot(qb, state.astype(BF16)) * rqd_ref[rows, cols]
            rstate_ref[hd] = state * rcd_ref[hd] + _dot_tn(rkdec_ref[rows, cols], vb)
            merged_ref[rows, cols] = (_head_norm(o, rnw_ref[:, cols]) * rg_ref[rows, cols]).astype(BF16)

    for c in range(HG_CHUNKS):
        rows = slice(c * HG_CHUNK, (c + 1) * HG_CHUNK)
        for hd in range(N_HEADS):
            cols = _head_cols(hd)
            p = hd // HEADS_PER_SCAN
            base = _scan_base(hd, c)
            q = hq_ref[rows, cols]
            k = hk_ref[rows, cols]
            v = hv_ref[rows, cols]
            vb = hvb_ref[rows, cols]
            a = a_ref[p, base:base + HG_CHUNK, :]
            b = b_ref[p, base:base + HG_CHUNK, :]
            b_row = lambda r: b_ref[p, base + r:base + r + 1, :]
            s = jnp.zeros((HG_CHUNK, HG_CHUNK), F32)
            for li, xl in enumerate(_hg_level_factors(q, k, a, b, b_row)):
                s = s + _dot_nt(xl, xl) * lmask_ref[li]
            state_t = hstate_ref[hd]
            o = _dot(s.astype(BF16), vb) + jnp.sum(q * k, axis=-1, keepdims=True) * v
            o = o + _dot_nt((q * jnp.exp2(b)).astype(BF16), state_t.astype(BF16))
            b_last = b_row(HG_CHUNK - 1)
            kd = (k * jnp.exp2(b_last - b)).astype(BF16)
            hstate_ref[hd] = state_t * jnp.exp2(b_last) + _dot_tn(vb, kd)
            merged_ref[rows, GROUP_W + hd * HEAD_DIM:GROUP_W + (hd + 1) * HEAD_DIM] = (
                _head_norm(o, hnw_ref[:, cols]) * hg_ref[rows, cols]).astype(BF16)

    o_ref[...] = x + mod[2:3] * _dot(merged_ref[...], wout_ref[...])


def _mixer_layer(x, mod_l, n1, win, wout, rnw, hnw, lb, cos2, sin2, rdm, rqd, rkd, rcd, lmask):
    bsz, seq, d = x.shape
    ts = SEQ_TILE
    grid = (bsz, seq // ts)
    x_spec = pl.BlockSpec((None, ts, d), lambda b, t: (b, t, 0))
    tab_spec = pl.BlockSpec((ts, HEAD_DIM), lambda b, t: (t, 0))
    group = lambda dtype: pltpu.VMEM((ts, GROUP_W), dtype)
    scan = pltpu.VMEM((N_HEADS // HEADS_PER_SCAN, SUBLANES * SCAN_PITCH, HEAD_DIM), F32)
    return pl.pallas_call(
        _mixer_kernel,
        grid=grid,
        in_specs=[
            x_spec,
            pl.BlockSpec((None, N_MOD, d), lambda b, t: (b, 0, 0)),
            _const_spec(n1.shape), _const_spec(win.shape), _const_spec(wout.shape),
            _const_spec(rnw.shape), _const_spec(hnw.shape), _const_spec(lb.shape),
            tab_spec, tab_spec,
            _const_spec(rdm.shape), _const_spec(rqd.shape), _const_spec(rkd.shape), _const_spec(rcd.shape),
            _const_spec(lmask.shape),
        ],
        out_specs=x_spec,
        out_shape=jax.ShapeDtypeStruct(x.shape, x.dtype),
        scratch_shapes=[
            group(BF16), group(BF16), group(BF16), group(BF16), group(F32),
            group(F32), group(F32), group(F32), group(BF16), group(F32), scan, scan,
            pltpu.VMEM((ts, 2 * GROUP_W), BF16),
            pltpu.VMEM((N_HEADS, HEAD_DIM, HEAD_DIM), F32),
            pltpu.VMEM((N_HEADS, HEAD_DIM, HEAD_DIM), F32),
        ],
        compiler_params=pltpu.CompilerParams(
            dimension_semantics=("arbitrary", "arbitrary"), vmem_limit_bytes=VMEM_LIMIT_BYTES),
        name="mixer",
    )(x, mod_l, n1, win, wout, rnw, hnw, lb, cos2, sin2, rdm, rqd, rkd, rcd, lmask)


def _ffn_kernel(*refs, final):
    if final:
        (x_ref, mod_ref, n2_ref, wg_ref, wv_ref, wd_ref, cw_ref, cb_ref, fw_ref,
         o_ref, a_ref, y_ref) = refs
    else:
        (x_ref, mod_ref, n2_ref, wg_ref, wv_ref, wd_ref, cw_ref, cb_ref,
         o_ref, a_ref, y_ref) = refs
    ts = x_ref.shape[0]

    @pl.when(pl.program_id(1) == 0)
    def _():
        a_ref[0:CARRY_ROWS, :] = jnp.zeros((CARRY_ROWS, D_FF), F32)

    @pl.when(pl.program_id(1) != 0)
    def _():
        a_ref[0:CARRY_ROWS, :] = a_ref[ts:ts + CARRY_ROWS, :]

    x = x_ref[...]
    mod = mod_ref[...]
    hb = _rms_mod(x, n2_ref[...], mod[4:5], mod[3:4]).astype(BF16)
    for j in range(D_FF // FF_BLOCK):
        cols = slice(j * FF_BLOCK, (j + 1) * FF_BLOCK)
        a_ref[CARRY_ROWS:CARRY_ROWS + ts, cols] = _dot(hb, wg_ref[:, cols])
        acc = cb_ref[:, cols] + a_ref[CARRY_ROWS - 2:CARRY_ROWS - 2 + ts, cols] * cw_ref[0:1, cols]
        acc = acc + a_ref[CARRY_ROWS - 1:CARRY_ROWS - 1 + ts, cols] * cw_ref[1:2, cols]
        acc = acc + a_ref[CARRY_ROWS:CARRY_ROWS + ts, cols] * cw_ref[2:3, cols]
        y_ref[:, cols] = (_silu(acc) * _dot(hb, wv_ref[:, cols])).astype(BF16)
    out = x + mod[5:6] * _dot(y_ref[...], wd_ref[...])
    if final:
        out = out * lax.rsqrt(jnp.mean(out * out, axis=-1, keepdims=True) + EPS) * fw_ref[...]
    o_ref[...] = out


def _ffn_layer(x, mod_l, n2, wg, wv, wd, cw, cb, final_w=None):
    bsz, seq, d = x.shape
    ts = SEQ_TILE
    final = final_w is not None
    x_spec = pl.BlockSpec((None, ts, d), lambda b, t: (b, t, 0))
    args = [x, mod_l, n2, wg, wv, wd, cw, cb] + ([final_w] if final else [])
    in_specs = [x_spec, pl.BlockSpec((None, N_MOD, d), lambda b, t: (b, 0, 0))]
    in_specs += [_const_spec(a.shape) for a in args[2:]]
    return pl.pallas_call(
        functools.partial(_ffn_kernel, final=final),
        grid=(bsz, seq // ts),
        in_specs=in_specs,
        out_specs=x_spec,
        out_shape=jax.ShapeDtypeStruct(x.shape, x.dtype),
        scratch_shapes=[
            pltpu.VMEM((ts + 2 * CARRY_ROWS, D_FF), F32),
            pltpu.VMEM((ts, D_FF), BF16),
        ],
        compiler_params=pltpu.CompilerParams(
            dimension_semantics=("arbitrary", "arbitrary"), vmem_limit_bytes=VMEM_LIMIT_BYTES),
        name="ffn",
    )(*args)


def _retention_tables():
    log_gamma = jnp.log(1.0 - jnp.exp2(-5.0 - jnp.arange(N_HEADS, dtype=F32)))
    lg = log_gamma[:, None, None]
    pos = jnp.arange(RET_CHUNK, dtype=F32)
    rel = pos[:, None] - pos[None, :]
    decay_mask = jnp.where(rel[None] >= 0, jnp.exp(lg * jnp.maximum(rel, 0.0)[None]), 0.0)
    ones = jnp.ones((1, 1, HEAD_DIM), F32)
    q_decay = jnp.exp(lg * (pos + 1.0)[None, :, None]) * ones
    k_decay = jnp.exp(lg * (RET_CHUNK - 1.0 - pos)[None, :, None]) * ones
    chunk_decay = jnp.exp(lg * RET_CHUNK) * jnp.ones((1, HEAD_DIM, HEAD_DIM), F32)

    def tile_layout(t):
        t = jnp.transpose(t, (1, 0, 2)).reshape(RET_CHUNK, GROUP_W)
        return jnp.tile(t, (SEQ_TILE // RET_CHUNK, 1))

    return decay_mask, tile_layout(q_decay), tile_layout(k_decay), chunk_decay


def kernel(x, c, w_in, w_out, ret_norm_w, hgrn_norm_w, hgrn_lb_logits, norm1_w, norm2_w,
           ada_w, ada_b, w_gate, w_val, conv_w, conv_b, w_down, final_norm_w):
    depth = w_in.shape[0]
    seq = x.shape[1]
    assert x.shape[2] == D_MODEL and seq % SEQ_TILE == 0 and SEQ_TILE % RET_CHUNK == 0
    assert HEADS_PER_SCAN * HG_CHUNKS == SUBLANES and N_HEADS % HEADS_PER_SCAN == 0

    mods = _ada_mods(c, ada_w, ada_b)
    cos2, sin2 = _rope_tables(seq)
    rdm, rqd, rkd, rcd = _retention_tables()
    lmask = jnp.asarray(_hg_level_masks())
    p = jax.nn.softmax(hgrn_lb_logits.astype(F32), axis=0)
    lower_bounds = jnp.cumsum(p, axis=0) - p[0:1]

    row = lambda v: v.reshape(1, -1)
    for l in range(depth):
        x = _mixer_layer(x, mods[l], row(norm1_w[l]), w_in[l].astype(BF16), w_out[l].astype(BF16),
                         row(ret_norm_w[l]), row(hgrn_norm_w[l]), row(lower_bounds[l]),
                         cos2, sin2, rdm, rqd, rkd, rcd, lmask)
        x = _ffn_layer(x, mods[l], row(norm2_w[l]), w_gate[l].astype(BF16), w_val[l].astype(BF16),
                       w_down[l].astype(BF16), conv_w[l], row(conv_b[l]),
                       final_w=row(final_norm_w) if l == depth - 1 else None)
    return x
```

```python
import functools

import numpy as np
import jax
import jax.numpy as jnp
from jax import lax
from jax.experimental import pallas as pl
from jax.experimental.pallas import tpu as pltpu

F32 = jnp.float32
BF16 = jnp.bfloat16

SUBLANES = 8
D_MODEL = 1024
HEAD_DIM = 128
N_HEADS = 4
GROUP_W = N_HEADS * HEAD_DIM
IN_COLS = 8 * GROUP_W
D_FF = 2816
CONV_WIDTH = 3
ROPE_BASE = 10000.0
EPS = 1e-6
N_MOD = 6

SEQ_TILE = 512
RET_CHUNK = 256
HG_CHUNK = 128
HG_LEVELS = (64, 32, 16, 8, 4, 2, 1)
HG_FACTORS = len(HG_LEVELS) + 2
HG_CHUNKS = SEQ_TILE // HG_CHUNK
HEADS_PER_SCAN = SUBLANES // HG_CHUNKS
SCAN_PITCH = HG_CHUNK + SUBLANES
GROUP_ORDER = (5, 4, 6, 7, 0, 1, 2, 3)
FF_BLOCK = 256
CARRY_ROWS = 8
ADA_BLOCK = 1536
VMEM_LIMIT_BYTES = 56 * 1024 * 1024


def _const_spec(shape):
    nd = len(shape)
    return pl.BlockSpec(shape, lambda *_: (0,) * nd, pipeline_mode=pl.Buffered(1))


def _layer_spec(stacked, layer):
    nd = stacked.ndim - 1
    return pl.BlockSpec((None,) + stacked.shape[1:], lambda *_: (layer,) + (0,) * nd,
                        pipeline_mode=pl.Buffered(1))


def _sigmoid(z):
    return 1.0 / (1.0 + jnp.exp(-z))


def _silu(z):
    return z * _sigmoid(z)


def _dot(a, b):
    return jnp.dot(a, b, preferred_element_type=F32)


def _dot_nt(a, b):
    return lax.dot_general(a, b, (((1,), (1,)), ((), ())), preferred_element_type=F32)


def _dot_tn(a, b):
    return lax.dot_general(a, b, (((0,), (0,)), ((), ())), preferred_element_type=F32)


def _ada_kernel(c_ref, w_ref, b_ref, o_ref):
    c = c_ref[...]
    o_ref[...] = jnp.dot(_silu(c), w_ref[...], preferred_element_type=F32,
                         precision=lax.Precision.HIGHEST) + b_ref[...]


def _ada_mods(c, ada_w, ada_b):
    depth, d, n = ada_w.shape
    b = c.shape[0]
    rows = SUBLANES
    c_pad = jnp.zeros((rows, d), F32).at[:b].set(c)
    out = pl.pallas_call(
        _ada_kernel,
        grid=(depth, n // ADA_BLOCK),
        in_specs=[
            pl.BlockSpec((rows, d), lambda l, j: (0, 0)),
            pl.BlockSpec((None, d, ADA_BLOCK), lambda l, j: (l, 0, j)),
            pl.BlockSpec((None, 1, ADA_BLOCK), lambda l, j: (l, 0, j)),
        ],
        out_specs=pl.BlockSpec((None, rows, ADA_BLOCK), lambda l, j: (l, 0, j)),
        out_shape=jax.ShapeDtypeStruct((depth, rows, n), F32),
        name="ada_mod",
    )(c_pad, ada_w, ada_b.reshape(depth, 1, n))
    return out[:, :b].reshape(depth, b, N_MOD, d)


def _rope_kernel(f_ref, cos_ref, sin_ref):
    rows = cos_ref.shape[0]
    pos = (lax.broadcasted_iota(jnp.int32, (rows, HEAD_DIM), 0) + pl.program_id(0) * rows).astype(F32)
    ang = pos * f_ref[...]
    lane = lax.broadcasted_iota(jnp.int32, (rows, HEAD_DIM), 1)
    cos_ref[...] = jnp.cos(ang)
    sin_ref[...] = jnp.where(lane < HEAD_DIM // 2, -jnp.sin(ang), jnp.sin(ang))


def _rope_tables(seq):
    inv_freq = ROPE_BASE ** (-jnp.arange(0, HEAD_DIM, 2, dtype=F32) / HEAD_DIM)
    f2 = jnp.concatenate([inv_freq, inv_freq]).reshape(1, HEAD_DIM)
    rows = 2048
    return pl.pallas_call(
        _rope_kernel,
        grid=(seq // rows,),
        in_specs=[pl.BlockSpec((1, HEAD_DIM), lambda i: (0, 0))],
        out_specs=[pl.BlockSpec((rows, HEAD_DIM), lambda i: (i, 0))] * 2,
        out_shape=[jax.ShapeDtypeStruct((seq, HEAD_DIM), F32)] * 2,
        name="rope_tables",
    )(f2)


def _hg_level_masks():
    c = HG_CHUNK
    i = np.arange(c)[:, None]
    j = np.arange(c)[None, :]
    out = []
    for m in HG_LEVELS:
        blk = 2 * m
        out.append(((i // blk == j // blk) & (i % blk >= m) & (j % blk < m)).astype(np.float32))
    return np.stack(out)


def _rms_mod(x, w, scale, shift):
    y = x * lax.rsqrt(jnp.mean(x * x, axis=-1, keepdims=True) + EPS)
    return y * (w * (1.0 + scale)) + shift


def _head_norm(o, w):
    return o * lax.rsqrt(jnp.mean(o * o, axis=-1, keepdims=True) + EPS) * w


def _head_cols(hd):
    return slice(hd * HEAD_DIM, (hd + 1) * HEAD_DIM)


def _scan_base(hd, c):
    return ((hd % HEADS_PER_SCAN) * HG_CHUNKS + c) * SCAN_PITCH


def _hg_level_factors(q, k, a, b, b_row):
    c = q.shape[0]
    sub = lax.broadcasted_iota(jnp.int32, q.shape, 0) % SUBLANES
    out = []
    for m in HG_LEVELS:
        blk = 2 * m
        if m >= SUBLANES:
            parts = []
            for base in range(0, c, blk):
                ref = b_row(base + m - 1)
                lo, up = slice(base, base + m), slice(base + m, base + blk)
                parts.append(k[lo] * jnp.exp2(ref - b[lo]))
                parts.append(q[up] * jnp.exp2(b[up] - ref))
            x = jnp.concatenate(parts, axis=0)
        elif m == 1:
            x = jnp.where(sub % 2 == 1, q * jnp.exp2(a), k)
        else:
            refs = []
            for base in range(0, c, SUBLANES):
                r = None
                for off in range(m - 1, SUBLANES, blk):
                    row = jnp.broadcast_to(b_row(base + off), (SUBLANES, HEAD_DIM))
                    r = row if r is None else jnp.where(sub[:SUBLANES] < off - m + 1, r, row)
                refs.append(r)
            e = b - jnp.concatenate(refs, axis=0)
            x = jnp.where(sub % blk >= m, q, k) * jnp.exp2(jnp.minimum(e, -e))
        out.append(x.astype(BF16))
    return out


def _mixer_kernel(x_ref, mod_ref, n1_ref, win_ref, wout_ref, rnw_ref, hnw_ref, lb_ref,
                  cos_ref, sin_ref, rdm_ref, rqd_ref, rkd_ref, rcd_ref, lmask_ref,
                  o_ref,
                  hb_ref, rq_ref, rk_ref, rkdec_ref, rv_ref, rg_ref,
                  hq_ref, hk_ref, hv_ref, hvb_ref, hg_ref, a_ref, b_ref, xf_ref,
                  hs_ref, rs_ref, merged_ref, rstate_ref, hstate_ref):
    ts = x_ref.shape[0]

    @pl.when(pl.program_id(1) == 0)
    def _():
        rstate_ref[...] = jnp.zeros_like(rstate_ref)
        hstate_ref[...] = jnp.zeros_like(hstate_ref)

    def proj(group):
        return _dot(hb_ref[...], win_ref[:, group * GROUP_W:(group + 1) * GROUP_W])

    def rotary(t):
        parts = []
        for hd in range(N_HEADS):
            th = t[:, _head_cols(hd)]
            parts.append(th * cos_ref[...] + pltpu.roll(th, HEAD_DIM // 2, 1) * sin_ref[...])
        return jnp.concatenate(parts, axis=1)

    def project(group):
        if group == 0:
            rq_ref[...] = rotary(proj(0)).astype(BF16)
        elif group == 1:
            rk = rotary(proj(1)) * (HEAD_DIM ** -0.5)
            rk_ref[...] = rk.astype(BF16)
            rkdec_ref[...] = (rk * rkd_ref[...]).astype(BF16)
        elif group == 2:
            rv_ref[...] = proj(2).astype(BF16)
        elif group == 3:
            rg_ref[...] = _silu(proj(3))
        elif group == 4:
            hq_ref[...] = _silu(proj(4))
        elif group == 5:
            z = proj(5)
            lb = lb_ref[...]
            e = jnp.exp(-jnp.abs(z))
            r = 1.0 / (1.0 + e)
            er = e * r
            a2 = jnp.log2(lb + (1.0 - lb) * jnp.where(z >= 0, r, er))
            hk_ref[...] = (1.0 - lb) * jnp.where(z >= 0, er, r)
            for hd in range(N_HEADS):
                for c in range(HG_CHUNKS):
                    a_ref[hd // HEADS_PER_SCAN, pl.ds(_scan_base(hd, c), HG_CHUNK), :] = (
                        a2[c * HG_CHUNK:(c + 1) * HG_CHUNK, _head_cols(hd)])
        elif group == 6:
            hv = proj(6)
            hv_ref[...] = hv
            hvb_ref[...] = hv.astype(BF16)
        else:
            hg_ref[...] = _silu(proj(7))

    def scan():
        for p in range(N_HEADS // HEADS_PER_SCAN):
            run = jnp.zeros((SUBLANES, HEAD_DIM), F32)
            for t in range(HG_CHUNK):
                rows = pl.ds(t, SUBLANES, stride=SCAN_PITCH)
                run = run + a_ref[p, rows, :]
                b_ref[p, rows, :] = run

    def hg_factors(c, hd):
        rows = slice(c * HG_CHUNK, (c + 1) * HG_CHUNK)
        cols = _head_cols(hd)
        p = hd // HEADS_PER_SCAN
        base = _scan_base(hd, c)
        q = hq_ref[rows, cols]
        k = hk_ref[rows, cols]
        a = a_ref[p, base:base + HG_CHUNK, :]
        b = b_ref[p, base:base + HG_CHUNK, :]
        b_row = lambda r: b_ref[p, base + r:base + r + 1, :]
        factors = _hg_level_factors(q, k, a, b, b_row)
        factors.append((q * jnp.exp2(b)).astype(BF16))
        factors.append((k * jnp.exp2(b_row(HG_CHUNK - 1) - b)).astype(BF16))
        for i, f in enumerate(factors):
            xf_ref[i, rows, cols] = f

    def hg_scores(c):
        for hd in range(N_HEADS):
            rows = slice(c * HG_CHUNK, (c + 1) * HG_CHUNK)
            cols = _head_cols(hd)
            s = jnp.zeros((HG_CHUNK, HG_CHUNK), F32)
            for li in range(len(HG_LEVELS)):
                xl = xf_ref[li, rows, cols]
                s = s + _dot_nt(xl, xl) * lmask_ref[li]
            hs_ref[(c % 2) * N_HEADS + hd] = s.astype(BF16)

    def hg_matmuls(c):
        out = []
        for hd in range(N_HEADS):
            rows = slice(c * HG_CHUNK, (c + 1) * HG_CHUNK)
            cols = _head_cols(hd)
            vb = hvb_ref[rows, cols]
            intra = _dot(hs_ref[(c % 2) * N_HEADS + hd], vb)
            inter = _dot_nt(xf_ref[HG_FACTORS - 2, rows, cols], hstate_ref[hd].astype(BF16))
            update = _dot_tn(vb, xf_ref[HG_FACTORS - 1, rows, cols])
            out.append((intra, inter, update))
        return out

    def hg_finish(c, results):
        for hd, (intra, inter, update) in enumerate(results):
            rows = slice(c * HG_CHUNK, (c + 1) * HG_CHUNK)
            cols = _head_cols(hd)
            last = _scan_base(hd, c) + HG_CHUNK - 1
            b_last = b_ref[hd // HEADS_PER_SCAN, last:last + 1, :]
            diag = jnp.sum(hq_ref[rows, cols] * hk_ref[rows, cols], axis=-1, keepdims=True)
            o = intra + diag * hv_ref[rows, cols] + inter
            hstate_ref[hd] = hstate_ref[hd] * jnp.exp2(b_last) + update
            merged_ref[rows, GROUP_W + hd * HEAD_DIM:GROUP_W + (hd + 1) * HEAD_DIM] = (
                _head_norm(o, hnw_ref[:, cols]) * hg_ref[rows, cols]).astype(BF16)

    def ret_scores(c):
        for hd in range(N_HEADS):
            rows = slice(c * RET_CHUNK, (c + 1) * RET_CHUNK)
            cols = _head_cols(hd)
            rs_ref[c * N_HEADS + hd] = (_dot_nt(rq_ref[rows, cols], rk_ref[rows, cols]) * rdm_ref[hd]).astype(BF16)

    def ret_matmuls(c):
        out = []
        for hd in range(N_HEADS):
            rows = slice(c * RET_CHUNK, (c + 1) * RET_CHUNK)
            cols = _head_cols(hd)
            vb = rv_ref[rows, cols]
            intra = _dot(rs_ref[c * N_HEADS + hd], vb)
            inter = _dot(rq_ref[rows, cols], rstate_ref[hd].astype(BF16))
            update = _dot_tn(rkdec_ref[rows, cols], vb)
            out.append((intra, inter, update))
        return out

    def ret_finish(c, results):
        for hd, (intra, inter, update) in enumerate(results):
            rows = slice(c * RET_CHUNK, (c + 1) * RET_CHUNK)
            cols = _head_cols(hd)
            o = intra + inter * rqd_ref[rows, cols]
            rstate_ref[hd] = rstate_ref[hd] * rcd_ref[hd] + update
            merged_ref[rows, cols] = (_head_norm(o, rnw_ref[:, cols]) * rg_ref[rows, cols]).astype(BF16)

    hb_ref[...] = _rms_mod(x_ref[...], n1_ref[...], mod_ref[1:2, :], mod_ref[0:1, :]).astype(BF16)

    chunk_heads = [(c, hd) for c in range(HG_CHUNKS) for hd in range(N_HEADS)]
    project(GROUP_ORDER[0])
    project(GROUP_ORDER[1])
    scan()
    rest = GROUP_ORDER[2:]
    per_group = -(-len(chunk_heads) // len(rest))
    for i, group in enumerate(rest):
        project(group)
        for c, hd in chunk_heads[i * per_group:(i + 1) * per_group]:
            hg_factors(c, hd)

    ret_scores(0)
    hg_scores(0)
    ret_scores(1)
    hg_scores(1)
    r0 = ret_matmuls(0)
    h0 = hg_matmuls(0)
    ret_finish(0, r0)
    hg_finish(0, h0)
    hg_scores(2)
    r1 = ret_matmuls(1)
    h1 = hg_matmuls(1)
    ret_finish(1, r1)
    hg_finish(1, h1)
    hg_scores(3)
    hg_finish(2, hg_matmuls(2))
    hg_finish(3, hg_matmuls(3))

    o_ref[...] = x_ref[...] + mod_ref[2:3, :] * _dot(merged_ref[...], wout_ref[...])


def _mixer_layer(layer, x, mod_l, n1, win, wout, rnw, hnw, lb, cos2, sin2, rdm, rqd, rkd, rcd, lmask):
    bsz, seq, d = x.shape
    ts = SEQ_TILE
    grid = (bsz, seq // ts)
    x_spec = pl.BlockSpec((None, ts, d), lambda b, t: (b, t, 0))
    tab_spec = pl.BlockSpec((ts, HEAD_DIM), lambda b, t: (t, 0))
    group = lambda dtype: pltpu.VMEM((ts, GROUP_W), dtype)
    scan = pltpu.VMEM((N_HEADS // HEADS_PER_SCAN, SUBLANES * SCAN_PITCH, HEAD_DIM), F32)
    return pl.pallas_call(
        _mixer_kernel,
        grid=grid,
        in_specs=[
            x_spec,
            pl.BlockSpec((None, N_MOD, d), lambda b, t: (b, 0, 0)),
            _const_spec(n1.shape), _layer_spec(win, layer), _layer_spec(wout, layer),
            _const_spec(rnw.shape), _const_spec(hnw.shape), _const_spec(lb.shape),
            tab_spec, tab_spec,
            _const_spec(rdm.shape), _const_spec(rqd.shape), _const_spec(rkd.shape), _const_spec(rcd.shape),
            _const_spec(lmask.shape),
        ],
        out_specs=x_spec,
        out_shape=jax.ShapeDtypeStruct(x.shape, x.dtype),
        scratch_shapes=[
            pltpu.VMEM((ts, d), BF16),
            group(BF16), group(BF16), group(BF16), group(BF16), group(F32),
            group(F32), group(F32), group(F32), group(BF16), group(F32), scan, scan,
            pltpu.VMEM((HG_FACTORS, ts, GROUP_W), BF16),
            pltpu.VMEM((2 * N_HEADS, HG_CHUNK, HG_CHUNK), BF16),
            pltpu.VMEM((ts // RET_CHUNK * N_HEADS, RET_CHUNK, RET_CHUNK), BF16),
            pltpu.VMEM((ts, 2 * GROUP_W), BF16),
            pltpu.VMEM((N_HEADS, HEAD_DIM, HEAD_DIM), F32),
            pltpu.VMEM((N_HEADS, HEAD_DIM, HEAD_DIM), F32),
        ],
        compiler_params=pltpu.CompilerParams(
            dimension_semantics=("arbitrary", "arbitrary"), vmem_limit_bytes=VMEM_LIMIT_BYTES),
        name="mixer",
    )(x, mod_l, n1, win, wout, rnw, hnw, lb, cos2, sin2, rdm, rqd, rkd, rcd, lmask)


def _ffn_kernel(*refs, final):
    if final:
        (x_ref, mod_ref, n2_ref, wg_ref, wv_ref, wd_ref, cw_ref, cb_ref, fw_ref,
         o_ref, a_ref, y_ref) = refs
    else:
        (x_ref, mod_ref, n2_ref, wg_ref, wv_ref, wd_ref, cw_ref, cb_ref,
         o_ref, a_ref, y_ref) = refs
    ts = x_ref.shape[0]

    @pl.when(pl.program_id(1) == 0)
    def _():
        a_ref[0:CARRY_ROWS, :] = jnp.zeros((CARRY_ROWS, D_FF), F32)

    @pl.when(pl.program_id(1) != 0)
    def _():
        a_ref[0:CARRY_ROWS, :] = a_ref[ts:ts + CARRY_ROWS, :]

    x = x_ref[...]
    mod = mod_ref[...]
    hb = _rms_mod(x, n2_ref[...], mod[4:5], mod[3:4]).astype(BF16)
    for j in range(D_FF // FF_BLOCK):
        cols = slice(j * FF_BLOCK, (j + 1) * FF_BLOCK)
        a_ref[CARRY_ROWS:CARRY_ROWS + ts, cols] = _dot(hb, wg_ref[:, cols])
        acc = cb_ref[:, cols] + a_ref[CARRY_ROWS - 2:CARRY_ROWS - 2 + ts, cols] * cw_ref[0:1, cols]
        acc = acc + a_ref[CARRY_ROWS - 1:CARRY_ROWS - 1 + ts, cols] * cw_ref[1:2, cols]
        acc = acc + a_ref[CARRY_ROWS:CARRY_ROWS + ts, cols] * cw_ref[2:3, cols]
        y_ref[:, cols] = (_silu(acc) * _dot(hb, wv_ref[:, cols])).astype(BF16)
    out = x + mod[5:6] * _dot(y_ref[...], wd_ref[...])
    if final:
        out = out * lax.rsqrt(jnp.mean(out * out, axis=-1, keepdims=True) + EPS) * fw_ref[...]
    o_ref[...] = out


def _ffn_layer(layer, x, mod_l, n2, wg, wv, wd, cw, cb, final_w=None):
    bsz, seq, d = x.shape
    ts = SEQ_TILE
    final = final_w is not None
    x_spec = pl.BlockSpec((None, ts, d), lambda b, t: (b, t, 0))
    args = [x, mod_l, n2, wg, wv, wd, cw, cb] + ([final_w] if final else [])
    in_specs = [x_spec, pl.BlockSpec((None, N_MOD, d), lambda b, t: (b, 0, 0)), _const_spec(n2.shape)]
    in_specs += [_layer_spec(w, layer) for w in (wg, wv, wd)]
    in_specs += [_const_spec(a.shape) for a in args[6:]]
    return pl.pallas_call(
        functools.partial(_ffn_kernel, final=final),
        grid=(bsz, seq // ts),
        in_specs=in_specs,
        out_specs=x_spec,
        out_shape=jax.ShapeDtypeStruct(x.shape, x.dtype),
        scratch_shapes=[
            pltpu.VMEM((ts + 2 * CARRY_ROWS, D_FF), F32),
            pltpu.VMEM((ts, D_FF), BF16),
        ],
        compiler_params=pltpu.CompilerParams(
            dimension_semantics=("arbitrary", "arbitrary"), vmem_limit_bytes=VMEM_LIMIT_BYTES),
        name="ffn",
    )(*args)


def _retention_tables():
    log_gamma = jnp.log(1.0 - jnp.exp2(-5.0 - jnp.arange(N_HEADS, dtype=F32)))
    lg = log_gamma[:, None, None]
    pos = jnp.arange(RET_CHUNK, dtype=F32)
    rel = pos[:, None] - pos[None, :]
    decay_mask = jnp.where(rel[None] >= 0, jnp.exp(lg * jnp.maximum(rel, 0.0)[None]), 0.0)
    ones = jnp.ones((1, 1, HEAD_DIM), F32)
    q_decay = jnp.exp(lg * (pos + 1.0)[None, :, None]) * ones
    k_decay = jnp.exp(lg * (RET_CHUNK - 1.0 - pos)[None, :, None]) * ones
    chunk_decay = jnp.exp(lg * RET_CHUNK) * jnp.ones((1, HEAD_DIM, HEAD_DIM), F32)

    def tile_layout(t):
        t = jnp.transpose(t, (1, 0, 2)).reshape(RET_CHUNK, GROUP_W)
        return jnp.tile(t, (SEQ_TILE // RET_CHUNK, 1))

    return decay_mask, tile_layout(q_decay), tile_layout(k_decay), chunk_decay


def kernel(x, c, w_in, w_out, ret_norm_w, hgrn_norm_w, hgrn_lb_logits, norm1_w, norm2_w,
           ada_w, ada_b, w_gate, w_val, conv_w, conv_b, w_down, final_norm_w):
    depth = w_in.shape[0]
    seq = x.shape[1]
    assert x.shape[2] == D_MODEL and seq % SEQ_TILE == 0 and SEQ_TILE % RET_CHUNK == 0
    assert HEADS_PER_SCAN * HG_CHUNKS == SUBLANES and N_HEADS % HEADS_PER_SCAN == 0

    mods = _ada_mods(c, ada_w, ada_b)
    cos2, sin2 = _rope_tables(seq)
    rdm, rqd, rkd, rcd = _retention_tables()
    lmask = jnp.asarray(_hg_level_masks())
    p = jax.nn.softmax(hgrn_lb_logits.astype(F32), axis=0)
    lower_bounds = jnp.cumsum(p, axis=0) - p[0:1]

    win, wout, wg, wv, wd = (w.astype(BF16) for w in (w_in, w_out, w_gate, w_val, w_down))
    row = lambda v: v.reshape(1, -1)
    for l in range(depth):
        x = _mixer_layer(l, x, mods[l], row(norm1_w[l]), win, wout,
                         row(ret_norm_w[l]), row(hgrn_norm_w[l]), row(lower_bounds[l]),
                         cos2, sin2, rdm, rqd, rkd, rcd, lmask)
        x = _ffn_layer(l, x, mods[l], row(norm2_w[l]), wg, wv, wd, conv_w[l], row(conv_b[l]),
                       final_w=row(final_norm_w) if l == depth - 1 else None)
    return x
```

```python
import functools

import numpy as np
import jax
import jax.numpy as jnp
from jax import lax
from jax.experimental import pallas as pl
from jax.experimental.pallas import tpu as pltpu

F32 = jnp.float32
BF16 = jnp.bfloat16

SUBLANES = 8
D_MODEL = 1024
HEAD_DIM = 128
N_HEADS = 4
GROUP_W = N_HEADS * HEAD_DIM
IN_COLS = 8 * GROUP_W
D_FF = 2816
CONV_WIDTH = 3
ROPE_BASE = 10000.0
EPS = 1e-6
N_MOD = 6

SEQ_TILE = 512
RET_CHUNK = 256
HG_CHUNK = 128
HG_LEVELS = (64, 32, 16, 8, 4, 2, 1)
HG_FACTORS = len(HG_LEVELS) + 2
HG_CHUNKS = SEQ_TILE // HG_CHUNK
FACTOR_ROWS = 16
HEADS_PER_SCAN = SUBLANES // HG_CHUNKS
SCAN_PITCH = HG_CHUNK + SUBLANES
GROUP_ORDER = (5, 4, 6, 7, 0, 1, 2, 3)
FF_BLOCK = 256
CARRY_ROWS = 8
ADA_BLOCK = 1536
VMEM_LIMIT_BYTES = 56 * 1024 * 1024


def _const_spec(shape):
    nd = len(shape)
    return pl.BlockSpec(shape, lambda *_: (0,) * nd, pipeline_mode=pl.Buffered(1))


def _layer_spec(stacked, layer):
    nd = stacked.ndim - 1
    return pl.BlockSpec((None,) + stacked.shape[1:], lambda *_: (layer,) + (0,) * nd,
                        pipeline_mode=pl.Buffered(1))


def _sigmoid(z):
    return 1.0 / (1.0 + jnp.exp(-z))


def _silu(z):
    return z * _sigmoid(z)


def _dot(a, b):
    return jnp.dot(a, b, preferred_element_type=F32)


def _dot_nt(a, b):
    return lax.dot_general(a, b, (((1,), (1,)), ((), ())), preferred_element_type=F32)


def _dot_tn(a, b):
    return lax.dot_general(a, b, (((0,), (0,)), ((), ())), preferred_element_type=F32)


def _ada_kernel(c_ref, w_ref, b_ref, o_ref):
    c = c_ref[...]
    o_ref[...] = jnp.dot(_silu(c), w_ref[...], preferred_element_type=F32,
                         precision=lax.Precision.HIGHEST) + b_ref[...]


def _ada_mods(c, ada_w, ada_b):
    depth, d, n = ada_w.shape
    b = c.shape[0]
    rows = SUBLANES
    c_pad = jnp.zeros((rows, d), F32).at[:b].set(c)
    out = pl.pallas_call(
        _ada_kernel,
        grid=(depth, n // ADA_BLOCK),
        in_specs=[
            pl.BlockSpec((rows, d), lambda l, j: (0, 0)),
            pl.BlockSpec((None, d, ADA_BLOCK), lambda l, j: (l, 0, j)),
            pl.BlockSpec((None, 1, ADA_BLOCK), lambda l, j: (l, 0, j)),
        ],
        out_specs=pl.BlockSpec((None, rows, ADA_BLOCK), lambda l, j: (l, 0, j)),
        out_shape=jax.ShapeDtypeStruct((depth, rows, n), F32),
        name="ada_mod",
    )(c_pad, ada_w, ada_b.reshape(depth, 1, n))
    return out[:, :b].reshape(depth, b, N_MOD, d)


def _rope_kernel(f_ref, cos_ref, sin_ref):
    rows = cos_ref.shape[0]
    pos = (lax.broadcasted_iota(jnp.int32, (rows, HEAD_DIM), 0) + pl.program_id(0) * rows).astype(F32)
    ang = pos * f_ref[...]
    lane = lax.broadcasted_iota(jnp.int32, (rows, HEAD_DIM), 1)
    cos_ref[...] = jnp.cos(ang)
    sin_ref[...] = jnp.where(lane < HEAD_DIM // 2, -jnp.sin(ang), jnp.sin(ang))


def _rope_tables(seq):
    inv_freq = ROPE_BASE ** (-jnp.arange(0, HEAD_DIM, 2, dtype=F32) / HEAD_DIM)
    f2 = jnp.concatenate([inv_freq, inv_freq]).reshape(1, HEAD_DIM)
    rows = 2048
    return pl.pallas_call(
        _rope_kernel,
        grid=(seq // rows,),
        in_specs=[pl.BlockSpec((1, HEAD_DIM), lambda i: (0, 0))],
        out_specs=[pl.BlockSpec((rows, HEAD_DIM), lambda i: (i, 0))] * 2,
        out_shape=[jax.ShapeDtypeStruct((seq, HEAD_DIM), F32)] * 2,
        name="rope_tables",
    )(f2)


def _hg_level_masks():
    c = HG_CHUNK
    i = np.arange(c)[:, None]
    j = np.arange(c)[None, :]
    out = []
    for m in HG_LEVELS:
        blk = 2 * m
        out.append(((i // blk == j // blk) & (i % blk >= m) & (j % blk < m)).astype(np.float32))
    return np.stack(out)


def _rms_mod(x, w, scale, shift):
    y = x * lax.rsqrt(jnp.mean(x * x, axis=-1, keepdims=True) + EPS)
    return y * (w * (1.0 + scale)) + shift


def _head_norm(o, w):
    return o * lax.rsqrt(jnp.mean(o * o, axis=-1, keepdims=True) + EPS) * w


def _head_cols(hd):
    return slice(hd * HEAD_DIM, (hd + 1) * HEAD_DIM)


def _scan_base(hd, c):
    return ((hd % HEADS_PER_SCAN) * HG_CHUNKS + c) * SCAN_PITCH


def _hg_level_factors(q, k, a, b, b_row, r0):
    n = q.shape[0]
    sub = lax.broadcasted_iota(jnp.int32, q.shape, 0) % SUBLANES
    out = []
    for m in HG_LEVELS:
        blk = 2 * m
        if m >= SUBLANES:
            parts = []
            for v in range(0, n, SUBLANES):
                sl = slice(v, v + SUBLANES)
                ref = b_row((r0 + v) // blk * blk + m - 1)
                if (r0 + v) % blk >= m:
                    parts.append(q[sl] * jnp.exp2(b[sl] - ref))
                else:
                    parts.append(k[sl] * jnp.exp2(ref - b[sl]))
            x = jnp.concatenate(parts, axis=0)
        elif m == 1:
            x = jnp.where(sub % 2 == 1, q * jnp.exp2(a), k)
        else:
            refs = []
            for v in range(0, n, SUBLANES):
                r = None
                for off in range(m - 1, SUBLANES, blk):
                    row = jnp.broadcast_to(b_row(r0 + v + off), (SUBLANES, HEAD_DIM))
                    r = row if r is None else jnp.where(sub[:SUBLANES] < off - m + 1, r, row)
                refs.append(r)
            e = b - jnp.concatenate(refs, axis=0)
            x = jnp.where(sub % blk >= m, q, k) * jnp.exp2(jnp.minimum(e, -e))
        out.append(x.astype(BF16))
    return out


def _mixer_kernel(x_ref, mod_ref, n1_ref, win_ref, wout_ref, rnw_ref, hnw_ref, lb_ref,
                  cos_ref, sin_ref, rdm_ref, rqd_ref, rkd_ref, rcd_ref, lmask_ref,
                  o_ref,
                  hb_ref, rq_ref, rk_ref, rkdec_ref, rv_ref, rg_ref,
                  hq_ref, hk_ref, hv_ref, hvb_ref, hg_ref, a_ref, b_ref, xf_ref,
                  hs_ref, rs_ref, xt_ref, merged_ref, rstate_ref, hstate_ref):
    ts = x_ref.shape[0]

    @pl.when(pl.program_id(1) == 0)
    def _():
        rstate_ref[...] = jnp.zeros_like(rstate_ref)
        hstate_ref[...] = jnp.zeros_like(hstate_ref)

    def proj(group):
        return _dot(hb_ref[...], win_ref[:, group * GROUP_W:(group + 1) * GROUP_W])

    def rotary(t):
        parts = []
        for hd in range(N_HEADS):
            th = t[:, _head_cols(hd)]
            parts.append(th * cos_ref[...] + pltpu.roll(th, HEAD_DIM // 2, 1) * sin_ref[...])
        return jnp.concatenate(parts, axis=1)

    def project(group):
        if group == 0:
            rq_ref[...] = rotary(proj(0)).astype(BF16)
        elif group == 1:
            rk = rotary(proj(1)) * (HEAD_DIM ** -0.5)
            rk_ref[...] = rk.astype(BF16)
            rkdec_ref[...] = (rk * rkd_ref[...]).astype(BF16)
        elif group == 2:
            rv_ref[...] = proj(2).astype(BF16)
        elif group == 3:
            rg_ref[...] = _silu(proj(3))
        elif group == 4:
            hq_ref[...] = _silu(proj(4))
        elif group == 5:
            z = proj(5)
            lb = lb_ref[...]
            e = jnp.exp(-jnp.abs(z))
            r = 1.0 / (1.0 + e)
            er = e * r
            a2 = jnp.log2(lb + (1.0 - lb) * jnp.where(z >= 0, r, er))
            hk_ref[...] = (1.0 - lb) * jnp.where(z >= 0, er, r)
            for hd in range(N_HEADS):
                for c in range(HG_CHUNKS):
                    a_ref[hd // HEADS_PER_SCAN, pl.ds(_scan_base(hd, c), HG_CHUNK), :] = (
                        a2[c * HG_CHUNK:(c + 1) * HG_CHUNK, _head_cols(hd)])
        elif group == 6:
            hv = proj(6)
            hv_ref[...] = hv
            hvb_ref[...] = hv.astype(BF16)
        else:
            hg_ref[...] = _silu(proj(7))

    def scan():
        for p in range(N_HEADS // HEADS_PER_SCAN):
            run = jnp.zeros((SUBLANES, HEAD_DIM), F32)
            for t in range(HG_CHUNK):
                rows = pl.ds(t, SUBLANES, stride=SCAN_PITCH)
                run = run + a_ref[p, rows, :]
                b_ref[p, rows, :] = run

    def hg_factors(c, hd):
        cols = _head_cols(hd)
        p = hd // HEADS_PER_SCAN
        base = _scan_base(hd, c)
        b_row = lambda r: b_ref[p, base + r:base + r + 1, :]
        for r0 in range(0, HG_CHUNK, FACTOR_ROWS):
            rows = slice(c * HG_CHUNK + r0, c * HG_CHUNK + r0 + FACTOR_ROWS)
            q = hq_ref[rows, cols]
            k = hk_ref[rows, cols]
            a = a_ref[p, base + r0:base + r0 + FACTOR_ROWS, :]
            b = b_ref[p, base + r0:base + r0 + FACTOR_ROWS, :]
            factors = _hg_level_factors(q, k, a, b, b_row, r0)
            factors.append((q * jnp.exp2(b)).astype(BF16))
            factors.append((k * jnp.exp2(b_row(HG_CHUNK - 1) - b)).astype(BF16))
            for i, f in enumerate(factors):
                xf_ref[i, rows, cols] = f
        rows = slice(c * HG_CHUNK, (c + 1) * HG_CHUNK)
        for li in range(len(HG_LEVELS)):
            xt_ref[li, c * N_HEADS + hd] = xf_ref[li, rows, cols].T

    def hg_scores(c):
        for hd in range(N_HEADS):
            rows = slice(c * HG_CHUNK, (c + 1) * HG_CHUNK)
            cols = _head_cols(hd)
            s = jnp.zeros((HG_CHUNK, HG_CHUNK), F32)
            for li in range(len(HG_LEVELS)):
                xl = xf_ref[li, rows, cols]
                s = s + _dot(xl, xt_ref[li, c * N_HEADS + hd]) * lmask_ref[li]
            hs_ref[c * N_HEADS + hd] = s.astype(BF16)

    def hg_matmuls(c):
        out = []
        for hd in range(N_HEADS):
            rows = slice(c * HG_CHUNK, (c + 1) * HG_CHUNK)
            cols = _head_cols(hd)
            vb = hvb_ref[rows, cols]
            intra = _dot(hs_ref[c * N_HEADS + hd], vb)
            inter = _dot_nt(xf_ref[HG_FACTORS - 2, rows, cols], hstate_ref[hd].astype(BF16))
            update = _dot_tn(vb, xf_ref[HG_FACTORS - 1, rows, cols])
            out.append((intra, inter, update))
        return out

    def hg_finish(c, results):
        for hd, (intra, inter, update) in enumerate(results):
            rows = slice(c * HG_CHUNK, (c + 1) * HG_CHUNK)
            cols = _head_cols(hd)
            last = _scan_base(hd, c) + HG_CHUNK - 1
            b_last = b_ref[hd // HEADS_PER_SCAN, last:last + 1, :]
            diag = jnp.sum(hq_ref[rows, cols] * hk_ref[rows, cols], axis=-1, keepdims=True)
            o = intra + diag * hv_ref[rows, cols] + inter
            hstate_ref[hd] = hstate_ref[hd] * jnp.exp2(b_last) + update
            merged_ref[rows, GROUP_W + hd * HEAD_DIM:GROUP_W + (hd + 1) * HEAD_DIM] = (
                _head_norm(o, hnw_ref[:, cols]) * hg_ref[rows, cols]).astype(BF16)

    def ret_scores(c):
        for hd in range(N_HEADS):
            rows = slice(c * RET_CHUNK, (c + 1) * RET_CHUNK)
            cols = _head_cols(hd)
            rs_ref[c * N_HEADS + hd] = (_dot_nt(rq_ref[rows, cols], rk_ref[rows, cols]) * rdm_ref[hd]).astype(BF16)

    def ret_matmuls(c):
        out = []
        for hd in range(N_HEADS):
            rows = slice(c * RET_CHUNK, (c + 1) * RET_CHUNK)
            cols = _head_cols(hd)
            vb = rv_ref[rows, cols]
            intra = _dot(rs_ref[c * N_HEADS + hd], vb)
            inter = _dot(rq_ref[rows, cols], rstate_ref[hd].astype(BF16))
            update = _dot_tn(rkdec_ref[rows, cols], vb)
            out.append((intra, inter, update))
        return out

    def ret_finish(c, results):
        for hd, (intra, inter, update) in enumerate(results):
            rows = slice(c * RET_CHUNK, (c + 1) * RET_CHUNK)
            cols = _head_cols(hd)
            o = intra + inter * rqd_ref[rows, cols]
            rstate_ref[hd] = rstate_ref[hd] * rcd_ref[hd] + update
            merged_ref[rows, cols] = (_head_norm(o, rnw_ref[:, cols]) * rg_ref[rows, cols]).astype(BF16)

    hb_ref[...] = _rms_mod(x_ref[...], n1_ref[...], mod_ref[1:2, :], mod_ref[0:1, :]).astype(BF16)

    chunk_heads = [(c, hd) for c in range(HG_CHUNKS) for hd in range(N_HEADS)]
    project(GROUP_ORDER[0])
    project(GROUP_ORDER[1])
    scan()
    rest = GROUP_ORDER[2:]
    per_group = -(-len(chunk_heads) // len(rest))
    for i, group in enumerate(rest):
        project(group)
        for c, hd in chunk_heads[i * per_group:(i + 1) * per_group]:
            hg_factors(c, hd)

    ret_scores(0)
    hg_scores(0)
    ret_scores(1)
    hg_scores(1)
    r0 = ret_matmuls(0)
    h0 = hg_matmuls(0)
    ret_finish(0, r0)
    hg_finish(0, h0)
    hg_scores(2)
    r1 = ret_matmuls(1)
    h1 = hg_matmuls(1)
    ret_finish(1, r1)
    hg_finish(1, h1)
    hg_scores(3)
    hg_finish(2, hg_matmuls(2))
    hg_finish(3, hg_matmuls(3))

    o_ref[...] = x_ref[...] + mod_ref[2:3, :] * _dot(merged_ref[...], wout_ref[...])


def _mixer_layer(layer, x, mod_l, n1, win, wout, rnw, hnw, lb, cos2, sin2, rdm, rqd, rkd, rcd, lmask):
    bsz, seq, d = x.shape
    ts = SEQ_TILE
    grid = (bsz, seq // ts)
    x_spec = pl.BlockSpec((None, ts, d), lambda b, t: (b, t, 0))
    tab_spec = pl.BlockSpec((ts, HEAD_DIM), lambda b, t: (t, 0))
    group = lambda dtype: pltpu.VMEM((ts, GROUP_W), dtype)
    scan = pltpu.VMEM((N_HEADS // HEADS_PER_SCAN, SUBLANES * SCAN_PITCH, HEAD_DIM), F32)
    return pl.pallas_call(
        _mixer_kernel,
        grid=grid,
        in_specs=[
            x_spec,
            pl.BlockSpec((None, N_MOD, d), lambda b, t: (b, 0, 0)),
            _const_spec(n1.shape), _layer_spec(win, layer), _layer_spec(wout, layer),
            _const_spec(rnw.shape), _const_spec(hnw.shape), _const_spec(lb.shape),
            tab_spec, tab_spec,
            _const_spec(rdm.shape), _const_spec(rqd.shape), _const_spec(rkd.shape), _const_spec(rcd.shape),
            _const_spec(lmask.shape),
        ],
        out_specs=x_spec,
        out_shape=jax.ShapeDtypeStruct(x.shape, x.dtype),
        scratch_shapes=[
            pltpu.VMEM((ts, d), BF16),
            group(BF16), group(BF16), group(BF16), group(BF16), group(F32),
            group(F32), group(F32), group(F32), group(BF16), group(F32), scan, scan,
            pltpu.VMEM((HG_FACTORS, ts, GROUP_W), BF16),
            pltpu.VMEM((HG_CHUNKS * N_HEADS, HG_CHUNK, HG_CHUNK), BF16),
            pltpu.VMEM((ts // RET_CHUNK * N_HEADS, RET_CHUNK, RET_CHUNK), BF16),
            pltpu.VMEM((len(HG_LEVELS), HG_CHUNKS * N_HEADS, HEAD_DIM, HG_CHUNK), BF16),
            pltpu.VMEM((ts, 2 * GROUP_W), BF16),
            pltpu.VMEM((N_HEADS, HEAD_DIM, HEAD_DIM), F32),
            pltpu.VMEM((N_HEADS, HEAD_DIM, HEAD_DIM), F32),
        ],
        compiler_params=pltpu.CompilerParams(
            dimension_semantics=("arbitrary", "arbitrary"), vmem_limit_bytes=VMEM_LIMIT_BYTES),
        name="mixer",
    )(x, mod_l, n1, win, wout, rnw, hnw, lb, cos2, sin2, rdm, rqd, rkd, rcd, lmask)


def _ffn_kernel(*refs, final):
    if final:
        (x_ref, mod_ref, n2_ref, wg_ref, wv_ref, wd_ref, cw_ref, cb_ref, fw_ref,
         o_ref, a_ref, y_ref) = refs
    else:
        (x_ref, mod_ref, n2_ref, wg_ref, wv_ref, wd_ref, cw_ref, cb_ref,
         o_ref, a_ref, y_ref) = refs
    ts = x_ref.shape[0]

    @pl.when(pl.program_id(1) == 0)
    def _():
        a_ref[0:CARRY_ROWS, :] = jnp.zeros((CARRY_ROWS, D_FF), F32)

    @pl.when(pl.program_id(1) != 0)
    def _():
        a_ref[0:CARRY_ROWS, :] = a_ref[ts:ts + CARRY_ROWS, :]

    x = x_ref[...]
    mod = mod_ref[...]
    hb = _rms_mod(x, n2_ref[...], mod[4:5], mod[3:4]).astype(BF16)
    for j in range(D_FF // FF_BLOCK):
        cols = slice(j * FF_BLOCK, (j + 1) * FF_BLOCK)
        a_ref[CARRY_ROWS:CARRY_ROWS + ts, cols] = _dot(hb, wg_ref[:, cols])
        acc = cb_ref[:, cols] + a_ref[CARRY_ROWS - 2:CARRY_ROWS - 2 + ts, cols] * cw_ref[0:1, cols]
        acc = acc + a_ref[CARRY_ROWS - 1:CARRY_ROWS - 1 + ts, cols] * cw_ref[1:2, cols]
        acc = acc + a_ref[CARRY_ROWS:CARRY_ROWS + ts, cols] * cw_ref[2:3, cols]
        y_ref[:, cols] = (_silu(acc) * _dot(hb, wv_ref[:, cols])).astype(BF16)
    out = x + mod[5:6] * _dot(y_ref[...], wd_ref[...])
    if final:
        out = out * lax.rsqrt(jnp.mean(out * out, axis=-1, keepdims=True) + EPS) * fw_ref[...]
    o_ref[...] = out


def _ffn_layer(layer, x, mod_l, n2, wg, wv, wd, cw, cb, final_w=None):
    bsz, seq, d = x.shape
    ts = SEQ_TILE
    final = final_w is not None
    x_spec = pl.BlockSpec((None, ts, d), lambda b, t: (b, t, 0))
    args = [x, mod_l, n2, wg, wv, wd, cw, cb] + ([final_w] if final else [])
    in_specs = [x_spec, pl.BlockSpec((None, N_MOD, d), lambda b, t: (b, 0, 0)), _const_spec(n2.shape)]
    in_specs += [_layer_spec(w, layer) for w in (wg, wv, wd)]
    in_specs += [_const_spec(a.shape) for a in args[6:]]
    return pl.pallas_call(
        functools.partial(_ffn_kernel, final=final),
        grid=(bsz, seq // ts),
        in_specs=in_specs,
        out_specs=x_spec,
        out_shape=jax.ShapeDtypeStruct(x.shape, x.dtype),
        scratch_shapes=[
            pltpu.VMEM((ts + 2 * CARRY_ROWS, D_FF), F32),
            pltpu.VMEM((ts, D_FF), BF16),
        ],
        compiler_params=pltpu.CompilerParams(
            dimension_semantics=("arbitrary", "arbitrary"), vmem_limit_bytes=VMEM_LIMIT_BYTES),
        name="ffn",
    )(*args)


def _retention_tables():
    log_gamma = jnp.log(1.0 - jnp.exp2(-5.0 - jnp.arange(N_HEADS, dtype=F32)))
    lg = log_gamma[:, None, None]
    pos = jnp.arange(RET_CHUNK, dtype=F32)
    rel = pos[:, None] - pos[None, :]
    decay_mask = jnp.where(rel[None] >= 0, jnp.exp(lg * jnp.maximum(rel, 0.0)[None]), 0.0)
    ones = jnp.ones((1, 1, HEAD_DIM), F32)
    q_decay = jnp.exp(lg * (pos + 1.0)[None, :, None]) * ones
    k_decay = jnp.exp(lg * (RET_CHUNK - 1.0 - pos)[None, :, None]) * ones
    chunk_decay = jnp.exp(lg * RET_CHUNK) * jnp.ones((1, HEAD_DIM, HEAD_DIM), F32)

    def tile_layout(t):
        t = jnp.transpose(t, (1, 0, 2)).reshape(RET_CHUNK, GROUP_W)
        return jnp.tile(t, (SEQ_TILE // RET_CHUNK, 1))

    return decay_mask, tile_layout(q_decay), tile_layout(k_decay), chunk_decay


def kernel(x, c, w_in, w_out, ret_norm_w, hgrn_norm_w, hgrn_lb_logits, norm1_w, norm2_w,
           ada_w, ada_b, w_gate, w_val, conv_w, conv_b, w_down, final_norm_w):
    depth = w_in.shape[0]
    seq = x.shape[1]
    assert x.shape[2] == D_MODEL and seq % SEQ_TILE == 0 and SEQ_TILE % RET_CHUNK == 0
    assert HEADS_PER_SCAN * HG_CHUNKS == SUBLANES and N_HEADS % HEADS_PER_SCAN == 0

    mods = _ada_mods(c, ada_w, ada_b)
    cos2, sin2 = _rope_tables(seq)
    rdm, rqd, rkd, rcd = _retention_tables()
    lmask = jnp.asarray(_hg_level_masks())
    p = jax.nn.softmax(hgrn_lb_logits.astype(F32), axis=0)
    lower_bounds = jnp.cumsum(p, axis=0) - p[0:1]

    win, wout, wg, wv, wd = (w.astype(BF16) for w in (w_in, w_out, w_gate, w_val, w_down))
    row = lambda v: v.reshape(1, -1)
    for l in range(depth):
        x = _mixer_layer(l, x, mods[l], row(norm1_w[l]), win, wout,
                         row(ret_norm_w[l]), row(hgrn_norm_w[l]), row(lower_bounds[l]),
                         cos2, sin2, rdm, rqd, rkd, rcd, lmask)
        x = _ffn_layer(l, x, mods[l], row(norm2_w[l]), wg, wv, wd, conv_w[l], row(conv_b[l]),
                       final_w=row(final_norm_w) if l == depth - 1 else None)
    return x
```

```python
import functools

import numpy as np
import jax
import jax.numpy as jnp
from jax import lax
from jax.experimental import pallas as pl
from jax.experimental.pallas import tpu as pltpu

F32 = jnp.float32
BF16 = jnp.bfloat16

SUBLANES = 8
D_MODEL = 1024
HEAD_DIM = 128
N_HEADS = 4
GROUP_W = N_HEADS * HEAD_DIM
IN_COLS = 8 * GROUP_W
D_FF = 2816
CONV_WIDTH = 3
ROPE_BASE = 10000.0
EPS = 1e-6
N_MOD = 6

SEQ_TILE = 512
RET_CHUNK = 256
HG_CHUNK = 128
HG_LEVELS = (64, 32, 16, 8, 4, 2, 1)
HG_FACTORS = len(HG_LEVELS) + 2
HG_CHUNKS = SEQ_TILE // HG_CHUNK
FACTOR_ROWS = 16
HEADS_PER_SCAN = SUBLANES // HG_CHUNKS
SCAN_PITCH = HG_CHUNK + SUBLANES
GROUP_ORDER = (5, 4, 6, 7, 0, 1, 2, 3)
FF_BLOCK = 256
CARRY_ROWS = 8
ADA_BLOCK = 1536
VMEM_LIMIT_BYTES = 56 * 1024 * 1024


def _const_spec(shape):
    nd = len(shape)
    return pl.BlockSpec(shape, lambda *_: (0,) * nd, pipeline_mode=pl.Buffered(1))


def _layer_spec(stacked, layer):
    nd = stacked.ndim - 1
    return pl.BlockSpec((None,) + stacked.shape[1:], lambda *_: (layer,) + (0,) * nd,
                        pipeline_mode=pl.Buffered(1))


def _sigmoid(z):
    return 1.0 / (1.0 + jnp.exp(-z))


def _silu(z):
    return z * _sigmoid(z)


def _dot(a, b):
    return jnp.dot(a, b, preferred_element_type=F32)


def _dot_nt(a, b):
    return lax.dot_general(a, b, (((1,), (1,)), ((), ())), preferred_element_type=F32)


def _dot_tn(a, b):
    return lax.dot_general(a, b, (((0,), (0,)), ((), ())), preferred_element_type=F32)


def _ada_kernel(c_ref, w_ref, b_ref, o_ref):
    c = c_ref[...]
    o_ref[...] = jnp.dot(_silu(c), w_ref[...], preferred_element_type=F32,
                         precision=lax.Precision.HIGHEST) + b_ref[...]


def _ada_mods(c, ada_w, ada_b):
    depth, d, n = ada_w.shape
    b = c.shape[0]
    rows = SUBLANES
    c_pad = jnp.zeros((rows, d), F32).at[:b].set(c)
    out = pl.pallas_call(
        _ada_kernel,
        grid=(depth, n // ADA_BLOCK),
        in_specs=[
            pl.BlockSpec((rows, d), lambda l, j: (0, 0)),
            pl.BlockSpec((None, d, ADA_BLOCK), lambda l, j: (l, 0, j)),
            pl.BlockSpec((None, 1, ADA_BLOCK), lambda l, j: (l, 0, j)),
        ],
        out_specs=pl.BlockSpec((None, rows, ADA_BLOCK), lambda l, j: (l, 0, j)),
        out_shape=jax.ShapeDtypeStruct((depth, rows, n), F32),
        name="ada_mod",
    )(c_pad, ada_w, ada_b.reshape(depth, 1, n))
    return out[:, :b].reshape(depth, b, N_MOD, d)


def _rope_kernel(f_ref, cos_ref, sin_ref):
    rows = cos_ref.shape[0]
    pos = (lax.broadcasted_iota(jnp.int32, (rows, HEAD_DIM), 0) + pl.program_id(0) * rows).astype(F32)
    ang = pos * f_ref[...]
    lane = lax.broadcasted_iota(jnp.int32, (rows, HEAD_DIM), 1)
    cos_ref[...] = jnp.cos(ang)
    sin_ref[...] = jnp.where(lane < HEAD_DIM // 2, -jnp.sin(ang), jnp.sin(ang))


def _rope_tables(seq):
    inv_freq = ROPE_BASE ** (-jnp.arange(0, HEAD_DIM, 2, dtype=F32) / HEAD_DIM)
    f2 = jnp.concatenate([inv_freq, inv_freq]).reshape(1, HEAD_DIM)
    rows = 2048
    return pl.pallas_call(
        _rope_kernel,
        grid=(seq // rows,),
        in_specs=[pl.BlockSpec((1, HEAD_DIM), lambda i: (0, 0))],
        out_specs=[pl.BlockSpec((rows, HEAD_DIM), lambda i: (i, 0))] * 2,
        out_shape=[jax.ShapeDtypeStruct((seq, HEAD_DIM), F32)] * 2,
        name="rope_tables",
    )(f2)


def _hg_level_masks():
    c = HG_CHUNK
    i = np.arange(c)[:, None]
    j = np.arange(c)[None, :]
    out = []
    for m in HG_LEVELS:
        blk = 2 * m
        out.append(((i // blk == j // blk) & (i % blk >= m) & (j % blk < m)).astype(np.float32))
    return np.stack(out)


def _rms_mod(x, w, scale, shift):
    y = x * lax.rsqrt(jnp.mean(x * x, axis=-1, keepdims=True) + EPS)
    return y * (w * (1.0 + scale)) + shift


def _head_norm(o, w):
    return o * lax.rsqrt(jnp.mean(o * o, axis=-1, keepdims=True) + EPS) * w


def _head_cols(hd):
    return slice(hd * HEAD_DIM, (hd + 1) * HEAD_DIM)


def _scan_base(hd, c):
    return ((hd % HEADS_PER_SCAN) * HG_CHUNKS + c) * SCAN_PITCH


def _hg_level_factors(q, k, a, b, b_row, r0):
    n = q.shape[0]
    sub = lax.broadcasted_iota(jnp.int32, q.shape, 0) % SUBLANES
    out = []
    for m in HG_LEVELS:
        blk = 2 * m
        if m >= SUBLANES:
            parts = []
            for v in range(0, n, SUBLANES):
                sl = slice(v, v + SUBLANES)
                ref = b_row((r0 + v) // blk * blk + m - 1)
                if (r0 + v) % blk >= m:
                    parts.append(q[sl] * jnp.exp2(b[sl] - ref))
                else:
                    parts.append(k[sl] * jnp.exp2(ref - b[sl]))
            x = jnp.concatenate(parts, axis=0)
        elif m == 1:
            x = jnp.where(sub % 2 == 1, q * jnp.exp2(a), k)
        else:
            refs = []
            for v in range(0, n, SUBLANES):
                r = None
                for off in range(m - 1, SUBLANES, blk):
                    row = jnp.broadcast_to(b_row(r0 + v + off), (SUBLANES, HEAD_DIM))
                    r = row if r is None else jnp.where(sub[:SUBLANES] < off - m + 1, r, row)
                refs.append(r)
            e = b - jnp.concatenate(refs, axis=0)
            x = jnp.where(sub % blk >= m, q, k) * jnp.exp2(jnp.minimum(e, -e))
        out.append(x.astype(BF16))
    return out


def _mixer_kernel(x_ref, mod_ref, n1_ref, win_ref, wout_ref, rnw_ref, hnw_ref, lb_ref,
                  cos_ref, sin_ref, rdm_ref, rqd_ref, rkd_ref, rcd_ref, lmask_ref,
                  o_ref,
                  hb_ref, rq_ref, rk_ref, rkdec_ref, rv_ref, rg_ref,
                  hq_ref, hk_ref, hv_ref, hvb_ref, hg_ref, a_ref, b_ref, xf_ref,
                  hs_ref, rs_ref, xt_ref, merged_ref, rstate_ref, hstate_ref):
    ts = x_ref.shape[0]

    @pl.when(pl.program_id(1) == 0)
    def _():
        rstate_ref[...] = jnp.zeros_like(rstate_ref)
        hstate_ref[...] = jnp.zeros_like(hstate_ref)

    def proj(group):
        return _dot(hb_ref[...], win_ref[:, group * GROUP_W:(group + 1) * GROUP_W])

    def rotary(t):
        parts = []
        for hd in range(N_HEADS):
            th = t[:, _head_cols(hd)]
            parts.append(th * cos_ref[...] + pltpu.roll(th, HEAD_DIM // 2, 1) * sin_ref[...])
        return jnp.concatenate(parts, axis=1)

    def project(group):
        if group == 0:
            rq_ref[...] = rotary(proj(0)).astype(BF16)
        elif group == 1:
            rk = rotary(proj(1)) * (HEAD_DIM ** -0.5)
            for hd in range(N_HEADS):
                rkt = rk[:, _head_cols(hd)].T
                rk_ref[hd] = rkt.astype(BF16)
                rkdec_ref[hd] = (rkt * rkd_ref[hd]).astype(BF16)
        elif group == 2:
            rv_ref[...] = proj(2).astype(BF16)
        elif group == 3:
            rg_ref[...] = _silu(proj(3))
        elif group == 4:
            hq_ref[...] = _silu(proj(4))
        elif group == 5:
            z = proj(5)
            lb = lb_ref[...]
            e = jnp.exp(-jnp.abs(z))
            r = 1.0 / (1.0 + e)
            er = e * r
            a2 = jnp.log2(lb + (1.0 - lb) * jnp.where(z >= 0, r, er))
            hk_ref[...] = (1.0 - lb) * jnp.where(z >= 0, er, r)
            for hd in range(N_HEADS):
                for c in range(HG_CHUNKS):
                    a_ref[hd // HEADS_PER_SCAN, pl.ds(_scan_base(hd, c), HG_CHUNK), :] = (
                        a2[c * HG_CHUNK:(c + 1) * HG_CHUNK, _head_cols(hd)])
        elif group == 6:
            hv = proj(6)
            hv_ref[...] = hv
            hvb_ref[...] = hv.astype(BF16)
        else:
            hg_ref[...] = _silu(proj(7))

    def scan():
        for p in range(N_HEADS // HEADS_PER_SCAN):
            run = jnp.zeros((SUBLANES, HEAD_DIM), F32)
            for t in range(HG_CHUNK):
                rows = pl.ds(t, SUBLANES, stride=SCAN_PITCH)
                run = run + a_ref[p, rows, :]
                b_ref[p, rows, :] = run

    def hg_factors(c, hd):
        cols = _head_cols(hd)
        p = hd // HEADS_PER_SCAN
        base = _scan_base(hd, c)
        b_row = lambda r: b_ref[p, base + r:base + r + 1, :]
        for r0 in range(0, HG_CHUNK, FACTOR_ROWS):
            rows = slice(c * HG_CHUNK + r0, c * HG_CHUNK + r0 + FACTOR_ROWS)
            q = hq_ref[rows, cols]
            k = hk_ref[rows, cols]
            a = a_ref[p, base + r0:base + r0 + FACTOR_ROWS, :]
            b = b_ref[p, base + r0:base + r0 + FACTOR_ROWS, :]
            factors = _hg_level_factors(q, k, a, b, b_row, r0)
            factors.append((q * jnp.exp2(b)).astype(BF16))
            factors.append((k * jnp.exp2(b_row(HG_CHUNK - 1) - b)).astype(BF16))
            for i, f in enumerate(factors):
                xf_ref[i, rows, cols] = f
        rows = slice(c * HG_CHUNK, (c + 1) * HG_CHUNK)
        for i, src in enumerate(list(range(len(HG_LEVELS))) + [HG_FACTORS - 1]):
            xt_ref[i, c * N_HEADS + hd] = xf_ref[src, rows, cols].T

    def hg_scores(c):
        for hd in range(N_HEADS):
            rows = slice(c * HG_CHUNK, (c + 1) * HG_CHUNK)
            cols = _head_cols(hd)
            s = jnp.zeros((HG_CHUNK, HG_CHUNK), F32)
            for li in range(len(HG_LEVELS)):
                xl = xf_ref[li, rows, cols]
                s = s + _dot(xl, xt_ref[li, c * N_HEADS + hd]) * lmask_ref[li]
            hs_ref[c * N_HEADS + hd] = s.astype(BF16)

    def hg_matmuls(c):
        out = []
        for hd in range(N_HEADS):
            rows = slice(c * HG_CHUNK, (c + 1) * HG_CHUNK)
            cols = _head_cols(hd)
            vb = hvb_ref[rows, cols]
            intra = _dot(hs_ref[c * N_HEADS + hd], vb)
            inter = _dot(xf_ref[HG_FACTORS - 2, rows, cols], hstate_ref[hd].astype(BF16))
            update = _dot(xt_ref[len(HG_LEVELS), c * N_HEADS + hd], vb)
            out.append((intra, inter, update))
        return out

    def hg_finish(c, results):
        for hd, (intra, inter, update) in enumerate(results):
            rows = slice(c * HG_CHUNK, (c + 1) * HG_CHUNK)
            cols = _head_cols(hd)
            last = _scan_base(hd, c) + HG_CHUNK - 1
            b_last = b_ref[hd // HEADS_PER_SCAN, last:last + 1, :]
            diag = jnp.sum(hq_ref[rows, cols] * hk_ref[rows, cols], axis=-1, keepdims=True)
            o = intra + diag * hv_ref[rows, cols] + inter
            decay = jnp.exp2(jnp.broadcast_to(b_last, (HEAD_DIM, HEAD_DIM)).T)
            hstate_ref[hd] = hstate_ref[hd] * decay + update
            merged_ref[rows, GROUP_W + hd * HEAD_DIM:GROUP_W + (hd + 1) * HEAD_DIM] = (
                _head_norm(o, hnw_ref[:, cols]) * hg_ref[rows, cols]).astype(BF16)

    def ret_scores(c):
        for hd in range(N_HEADS):
            rows = slice(c * RET_CHUNK, (c + 1) * RET_CHUNK)
            cols = _head_cols(hd)
            rs_ref[c * N_HEADS + hd] = (_dot(rq_ref[rows, cols], rk_ref[hd, :, rows]) * rdm_ref[hd]).astype(BF16)

    def ret_matmuls(c):
        out = []
        for hd in range(N_HEADS):
            rows = slice(c * RET_CHUNK, (c + 1) * RET_CHUNK)
            cols = _head_cols(hd)
            vb = rv_ref[rows, cols]
            intra = _dot(rs_ref[c * N_HEADS + hd], vb)
            inter = _dot(rq_ref[rows, cols], rstate_ref[hd].astype(BF16))
            update = _dot(rkdec_ref[hd, :, rows], vb)
            out.append((intra, inter, update))
        return out

    def ret_finish(c, results):
        for hd, (intra, inter, update) in enumerate(results):
            rows = slice(c * RET_CHUNK, (c + 1) * RET_CHUNK)
            cols = _head_cols(hd)
            o = intra + inter * rqd_ref[rows, cols]
            rstate_ref[hd] = rstate_ref[hd] * rcd_ref[hd] + update
            merged_ref[rows, cols] = (_head_norm(o, rnw_ref[:, cols]) * rg_ref[rows, cols]).astype(BF16)

    hb_ref[...] = _rms_mod(x_ref[...], n1_ref[...], mod_ref[1:2, :], mod_ref[0:1, :]).astype(BF16)

    chunk_heads = [(c, hd) for c in range(HG_CHUNKS) for hd in range(N_HEADS)]
    project(GROUP_ORDER[0])
    project(GROUP_ORDER[1])
    scan()
    rest = GROUP_ORDER[2:]
    per_group = -(-len(chunk_heads) // len(rest))
    for i, group in enumerate(rest):
        project(group)
        for c, hd in chunk_heads[i * per_group:(i + 1) * per_group]:
            hg_factors(c, hd)

    ret_scores(0)
    hg_scores(0)
    ret_scores(1)
    hg_scores(1)
    r0 = ret_matmuls(0)
    h0 = hg_matmuls(0)
    ret_finish(0, r0)
    hg_finish(0, h0)
    hg_scores(2)
    r1 = ret_matmuls(1)
    h1 = hg_matmuls(1)
    ret_finish(1, r1)
    hg_finish(1, h1)
    hg_scores(3)
    hg_finish(2, hg_matmuls(2))
    hg_finish(3, hg_matmuls(3))

    o_ref[...] = x_ref[...] + mod_ref[2:3, :] * _dot(merged_ref[...], wout_ref[...])


def _mixer_layer(layer, x, mod_l, n1, win, wout, rnw, hnw, lb, cos2, sin2, rdm, rqd, rkd, rcd, lmask):
    bsz, seq, d = x.shape
    ts = SEQ_TILE
    grid = (bsz, seq // ts)
    x_spec = pl.BlockSpec((None, ts, d), lambda b, t: (b, t, 0))
    tab_spec = pl.BlockSpec((ts, HEAD_DIM), lambda b, t: (t, 0))
    group = lambda dtype: pltpu.VMEM((ts, GROUP_W), dtype)
    scan = pltpu.VMEM((N_HEADS // HEADS_PER_SCAN, SUBLANES * SCAN_PITCH, HEAD_DIM), F32)
    keys_t = pltpu.VMEM((N_HEADS, HEAD_DIM, ts), BF16)
    return pl.pallas_call(
        _mixer_kernel,
        grid=grid,
        in_specs=[
            x_spec,
            pl.BlockSpec((None, N_MOD, d), lambda b, t: (b, 0, 0)),
            _const_spec(n1.shape), _layer_spec(win, layer), _layer_spec(wout, layer),
            _const_spec(rnw.shape), _const_spec(hnw.shape), _const_spec(lb.shape),
            tab_spec, tab_spec,
            _const_spec(rdm.shape), _const_spec(rqd.shape), _const_spec(rkd.shape), _const_spec(rcd.shape),
            _const_spec(lmask.shape),
        ],
        out_specs=x_spec,
        out_shape=jax.ShapeDtypeStruct(x.shape, x.dtype),
        scratch_shapes=[
            pltpu.VMEM((ts, d), BF16),
            group(BF16), keys_t, keys_t, group(BF16), group(F32),
            group(F32), group(F32), group(F32), group(BF16), group(F32), scan, scan,
            pltpu.VMEM((HG_FACTORS, ts, GROUP_W), BF16),
            pltpu.VMEM((HG_CHUNKS * N_HEADS, HG_CHUNK, HG_CHUNK), BF16),
            pltpu.VMEM((ts // RET_CHUNK * N_HEADS, RET_CHUNK, RET_CHUNK), BF16),
            pltpu.VMEM((len(HG_LEVELS) + 1, HG_CHUNKS * N_HEADS, HEAD_DIM, HG_CHUNK), BF16),
            pltpu.VMEM((ts, 2 * GROUP_W), BF16),
            pltpu.VMEM((N_HEADS, HEAD_DIM, HEAD_DIM), F32),
            pltpu.VMEM((N_HEADS, HEAD_DIM, HEAD_DIM), F32),
        ],
        compiler_params=pltpu.CompilerParams(
            dimension_semantics=("arbitrary", "arbitrary"), vmem_limit_bytes=VMEM_LIMIT_BYTES),
        name="mixer",
    )(x, mod_l, n1, win, wout, rnw, hnw, lb, cos2, sin2, rdm, rqd, rkd, rcd, lmask)


def _ffn_kernel(*refs, final):
    if final:
        (x_ref, mod_ref, n2_ref, wg_ref, wv_ref, wd_ref, cw_ref, cb_ref, fw_ref,
         o_ref, a_ref, y_ref) = refs
    else:
        (x_ref, mod_ref, n2_ref, wg_ref, wv_ref, wd_ref, cw_ref, cb_ref,
         o_ref, a_ref, y_ref) = refs
    ts = x_ref.shape[0]

    @pl.when(pl.program_id(1) == 0)
    def _():
        a_ref[0:CARRY_ROWS, :] = jnp.zeros((CARRY_ROWS, D_FF), F32)

    @pl.when(pl.program_id(1) != 0)
    def _():
        a_ref[0:CARRY_ROWS, :] = a_ref[ts:ts + CARRY_ROWS, :]

    x = x_ref[...]
    mod = mod_ref[...]
    hb = _rms_mod(x, n2_ref[...], mod[4:5], mod[3:4]).astype(BF16)
    for j in range(D_FF // FF_BLOCK):
        cols = slice(j * FF_BLOCK, (j + 1) * FF_BLOCK)
        a_ref[CARRY_ROWS:CARRY_ROWS + ts, cols] = _dot(hb, wg_ref[:, cols])
        acc = cb_ref[:, cols] + a_ref[CARRY_ROWS - 2:CARRY_ROWS - 2 + ts, cols] * cw_ref[0:1, cols]
        acc = acc + a_ref[CARRY_ROWS - 1:CARRY_ROWS - 1 + ts, cols] * cw_ref[1:2, cols]
        acc = acc + a_ref[CARRY_ROWS:CARRY_ROWS + ts, cols] * cw_ref[2:3, cols]
        y_ref[:, cols] = (_silu(acc) * _dot(hb, wv_ref[:, cols])).astype(BF16)
    out = x + mod[5:6] * _dot(y_ref[...], wd_ref[...])
    if final:
        out = out * lax.rsqrt(jnp.mean(out * out, axis=-1, keepdims=True) + EPS) * fw_ref[...]
    o_ref[...] = out


def _ffn_layer(layer, x, mod_l, n2, wg, wv, wd, cw, cb, final_w=None):
    bsz, seq, d = x.shape
    ts = SEQ_TILE
    final = final_w is not None
    x_spec = pl.BlockSpec((None, ts, d), lambda b, t: (b, t, 0))
    args = [x, mod_l, n2, wg, wv, wd, cw, cb] + ([final_w] if final else [])
    in_specs = [x_spec, pl.BlockSpec((None, N_MOD, d), lambda b, t: (b, 0, 0)), _const_spec(n2.shape)]
    in_specs += [_layer_spec(w, layer) for w in (wg, wv, wd)]
    in_specs += [_const_spec(a.shape) for a in args[6:]]
    return pl.pallas_call(
        functools.partial(_ffn_kernel, final=final),
        grid=(bsz, seq // ts),
        in_specs=in_specs,
        out_specs=x_spec,
        out_shape=jax.ShapeDtypeStruct(x.shape, x.dtype),
        scratch_shapes=[
            pltpu.VMEM((ts + 2 * CARRY_ROWS, D_FF), F32),
            pltpu.VMEM((ts, D_FF), BF16),
        ],
        compiler_params=pltpu.CompilerParams(
            dimension_semantics=("arbitrary", "arbitrary"), vmem_limit_bytes=VMEM_LIMIT_BYTES),
        name="ffn",
    )(*args)


def _retention_tables():
    log_gamma = jnp.log(1.0 - jnp.exp2(-5.0 - jnp.arange(N_HEADS, dtype=F32)))
    lg = log_gamma[:, None, None]
    pos = jnp.arange(RET_CHUNK, dtype=F32)
    rel = pos[:, None] - pos[None, :]
    decay_mask = jnp.where(rel[None] >= 0, jnp.exp(lg * jnp.maximum(rel, 0.0)[None]), 0.0)
    ones = jnp.ones((1, 1, HEAD_DIM), F32)
    q_decay = jnp.exp(lg * (pos + 1.0)[None, :, None]) * ones
    k_decay = jnp.exp(lg * (RET_CHUNK - 1.0 - pos)[None, :, None]) * ones
    chunk_decay = jnp.exp(lg * RET_CHUNK) * jnp.ones((1, HEAD_DIM, HEAD_DIM), F32)

    def tile_layout(t):
        t = jnp.transpose(t, (1, 0, 2)).reshape(RET_CHUNK, GROUP_W)
        return jnp.tile(t, (SEQ_TILE // RET_CHUNK, 1))

    k_decay_t = jnp.tile(k_decay[:, :, 0], (1, SEQ_TILE // RET_CHUNK))[:, None, :]
    return decay_mask, tile_layout(q_decay), k_decay_t, chunk_decay


def kernel(x, c, w_in, w_out, ret_norm_w, hgrn_norm_w, hgrn_lb_logits, norm1_w, norm2_w,
           ada_w, ada_b, w_gate, w_val, conv_w, conv_b, w_down, final_norm_w):
    depth = w_in.shape[0]
    seq = x.shape[1]
    assert x.shape[2] == D_MODEL and seq % SEQ_TILE == 0 and SEQ_TILE % RET_CHUNK == 0
    assert HEADS_PER_SCAN * HG_CHUNKS == SUBLANES and N_HEADS % HEADS_PER_SCAN == 0

    mods = _ada_mods(c, ada_w, ada_b)
    cos2, sin2 = _rope_tables(seq)
    rdm, rqd, rkd, rcd = _retention_tables()
    lmask = jnp.asarray(_hg_level_masks())
    p = jax.nn.softmax(hgrn_lb_logits.astype(F32), axis=0)
    lower_bounds = jnp.cumsum(p, axis=0) - p[0:1]

    win, wout, wg, wv, wd = (w.astype(BF16) for w in (w_in, w_out, w_gate, w_val, w_down))
    row = lambda v: v.reshape(1, -1)
    for l in range(depth):
        x = _mixer_layer(l, x, mods[l], row(norm1_w[l]), win, wout,
                         row(ret_norm_w[l]), row(hgrn_norm_w[l]), row(lower_bounds[l]),
                         cos2, sin2, rdm, rqd, rkd, rcd, lmask)
        x = _ffn_layer(l, x, mods[l], row(norm2_w[l]), wg, wv, wd, conv_w[l], row(conv_b[l]),
                       final_w=row(final_norm_w) if l == depth - 1 else None)
    return x
```

```python
import functools

import numpy as np
import jax
import jax.numpy as jnp
from jax import lax
from jax.experimental import pallas as pl
from jax.experimental.pallas import tpu as pltpu

F32 = jnp.float32
BF16 = jnp.bfloat16

SUBLANES = 8
D_MODEL = 1024
HEAD_DIM = 128
N_HEADS = 4
GROUP_W = N_HEADS * HEAD_DIM
IN_COLS = 8 * GROUP_W
D_FF = 2816
CONV_WIDTH = 3
ROPE_BASE = 10000.0
EPS = 1e-6
N_MOD = 6

SEQ_TILE = 512
FFN_TILE = 1024
RET_CHUNK = 256
HG_CHUNK = 128
HG_LEVELS = (64, 32, 16, 8, 4, 2, 1)
HG_FACTORS = len(HG_LEVELS) + 2
HG_CHUNKS = SEQ_TILE // HG_CHUNK
FACTOR_ROWS = 16
HEADS_PER_SCAN = SUBLANES // HG_CHUNKS
SCAN_PITCH = HG_CHUNK + SUBLANES
GROUP_ORDER = (5, 4, 6, 7, 0, 1, 2, 3)
FF_BLOCK = 256
CARRY_ROWS = 8
ADA_BLOCK = 1536
VMEM_LIMIT_BYTES = 56 * 1024 * 1024


def _const_spec(shape):
    nd = len(shape)
    return pl.BlockSpec(shape, lambda *_: (0,) * nd, pipeline_mode=pl.Buffered(1))


def _layer_spec(stacked, layer):
    nd = stacked.ndim - 1
    return pl.BlockSpec((None,) + stacked.shape[1:], lambda *_: (layer,) + (0,) * nd,
                        pipeline_mode=pl.Buffered(1))


def _sigmoid(z):
    return 1.0 / (1.0 + jnp.exp(-z))


def _silu(z):
    return z * _sigmoid(z)


def _dot(a, b):
    return jnp.dot(a, b, preferred_element_type=F32)


def _dot_nt(a, b):
    return lax.dot_general(a, b, (((1,), (1,)), ((), ())), preferred_element_type=F32)


def _dot_tn(a, b):
    return lax.dot_general(a, b, (((0,), (0,)), ((), ())), preferred_element_type=F32)


def _ada_kernel(c_ref, w_ref, b_ref, o_ref):
    c = c_ref[...]
    o_ref[...] = jnp.dot(_silu(c), w_ref[...], preferred_element_type=F32,
                         precision=lax.Precision.HIGHEST) + b_ref[...]


def _ada_mods(c, ada_w, ada_b):
    depth, d, n = ada_w.shape
    b = c.shape[0]
    rows = SUBLANES
    c_pad = jnp.zeros((rows, d), F32).at[:b].set(c)
    out = pl.pallas_call(
        _ada_kernel,
        grid=(depth, n // ADA_BLOCK),
        in_specs=[
            pl.BlockSpec((rows, d), lambda l, j: (0, 0)),
            pl.BlockSpec((None, d, ADA_BLOCK), lambda l, j: (l, 0, j)),
            pl.BlockSpec((None, 1, ADA_BLOCK), lambda l, j: (l, 0, j)),
        ],
        out_specs=pl.BlockSpec((None, rows, ADA_BLOCK), lambda l, j: (l, 0, j)),
        out_shape=jax.ShapeDtypeStruct((depth, rows, n), F32),
        name="ada_mod",
    )(c_pad, ada_w, ada_b.reshape(depth, 1, n))
    return out[:, :b].reshape(depth, b, N_MOD, d)


def _rope_kernel(f_ref, cos_ref, sin_ref):
    rows = cos_ref.shape[0]
    pos = (lax.broadcasted_iota(jnp.int32, (rows, HEAD_DIM), 0) + pl.program_id(0) * rows).astype(F32)
    ang = pos * f_ref[...]
    lane = lax.broadcasted_iota(jnp.int32, (rows, HEAD_DIM), 1)
    cos_ref[...] = jnp.cos(ang)
    sin_ref[...] = jnp.where(lane < HEAD_DIM // 2, -jnp.sin(ang), jnp.sin(ang))


def _rope_tables(seq):
    inv_freq = ROPE_BASE ** (-jnp.arange(0, HEAD_DIM, 2, dtype=F32) / HEAD_DIM)
    f2 = jnp.concatenate([inv_freq, inv_freq]).reshape(1, HEAD_DIM)
    rows = 2048
    return pl.pallas_call(
        _rope_kernel,
        grid=(seq // rows,),
        in_specs=[pl.BlockSpec((1, HEAD_DIM), lambda i: (0, 0))],
        out_specs=[pl.BlockSpec((rows, HEAD_DIM), lambda i: (i, 0))] * 2,
        out_shape=[jax.ShapeDtypeStruct((seq, HEAD_DIM), F32)] * 2,
        name="rope_tables",
    )(f2)


def _hg_level_masks():
    c = HG_CHUNK
    i = np.arange(c)[:, None]
    j = np.arange(c)[None, :]
    out = []
    for m in HG_LEVELS:
        blk = 2 * m
        out.append(((i // blk == j // blk) & (i % blk >= m) & (j % blk < m)).astype(np.float32))
    return np.stack(out)


def _rms_mod(x, w, scale, shift):
    y = x * lax.rsqrt(jnp.mean(x * x, axis=-1, keepdims=True) + EPS)
    return y * (w * (1.0 + scale)) + shift


def _head_norm(o, w):
    return o * lax.rsqrt(jnp.mean(o * o, axis=-1, keepdims=True) + EPS) * w


def _head_cols(hd):
    return slice(hd * HEAD_DIM, (hd + 1) * HEAD_DIM)


def _scan_base(hd, c):
    return ((hd % HEADS_PER_SCAN) * HG_CHUNKS + c) * SCAN_PITCH


def _hg_level_factors(q, k, a, b, b_row, r0):
    n = q.shape[0]
    sub = lax.broadcasted_iota(jnp.int32, q.shape, 0) % SUBLANES
    out = []
    for m in HG_LEVELS:
        blk = 2 * m
        if m >= SUBLANES:
            parts = []
            for v in range(0, n, SUBLANES):
                sl = slice(v, v + SUBLANES)
                ref = b_row((r0 + v) // blk * blk + m - 1)
                if (r0 + v) % blk >= m:
                    parts.append(q[sl] * jnp.exp2(b[sl] - ref))
                else:
                    parts.append(k[sl] * jnp.exp2(ref - b[sl]))
            x = jnp.concatenate(parts, axis=0)
        elif m == 1:
            x = jnp.where(sub % 2 == 1, q * jnp.exp2(a), k)
        else:
            refs = []
            for v in range(0, n, SUBLANES):
                r = None
                for off in range(m - 1, SUBLANES, blk):
                    row = jnp.broadcast_to(b_row(r0 + v + off), (SUBLANES, HEAD_DIM))
                    r = row if r is None else jnp.where(sub[:SUBLANES] < off - m + 1, r, row)
                refs.append(r)
            e = b - jnp.concatenate(refs, axis=0)
            x = jnp.where(sub % blk >= m, q, k) * jnp.exp2(jnp.minimum(e, -e))
        out.append(x.astype(BF16))
    return out


def _mixer_kernel(x_ref, mod_ref, n1_ref, win_ref, wout_ref, rnw_ref, hnw_ref, lb_ref,
                  cos_ref, sin_ref, rdm_ref, rqd_ref, rkd_ref, rcd_ref, lmask_ref,
                  o_ref,
                  hb_ref, rq_ref, rk_ref, rkdec_ref, rv_ref, rg_ref,
                  hq_ref, hk_ref, hv_ref, hvb_ref, hg_ref, a_ref, b_ref, xf_ref,
                  hs_ref, rs_ref, xt_ref, merged_ref, rstate_ref, hstate_ref):
    ts = x_ref.shape[0]

    @pl.when(pl.program_id(1) == 0)
    def _():
        rstate_ref[...] = jnp.zeros_like(rstate_ref)
        hstate_ref[...] = jnp.zeros_like(hstate_ref)

    def proj(group):
        return _dot(hb_ref[...], win_ref[:, group * GROUP_W:(group + 1) * GROUP_W])

    def rotary(t):
        parts = []
        for hd in range(N_HEADS):
            th = t[:, _head_cols(hd)]
            parts.append(th * cos_ref[...] + pltpu.roll(th, HEAD_DIM // 2, 1) * sin_ref[...])
        return jnp.concatenate(parts, axis=1)

    def project(group):
        if group == 0:
            rq_ref[...] = rotary(proj(0)).astype(BF16)
        elif group == 1:
            rk = rotary(proj(1)) * (HEAD_DIM ** -0.5)
            for hd in range(N_HEADS):
                rkt = rk[:, _head_cols(hd)].T
                rk_ref[hd] = rkt.astype(BF16)
                rkdec_ref[hd] = (rkt * rkd_ref[hd]).astype(BF16)
        elif group == 2:
            rv_ref[...] = proj(2).astype(BF16)
        elif group == 3:
            rg_ref[...] = _silu(proj(3))
        elif group == 4:
            hq_ref[...] = _silu(proj(4))
        elif group == 5:
            z = proj(5)
            lb = lb_ref[...]
            e = jnp.exp(-jnp.abs(z))
            r = 1.0 / (1.0 + e)
            er = e * r
            a2 = jnp.log2(lb + (1.0 - lb) * jnp.where(z >= 0, r, er))
            hk_ref[...] = (1.0 - lb) * jnp.where(z >= 0, er, r)
            for hd in range(N_HEADS):
                for c in range(HG_CHUNKS):
                    a_ref[hd // HEADS_PER_SCAN, pl.ds(_scan_base(hd, c), HG_CHUNK), :] = (
                        a2[c * HG_CHUNK:(c + 1) * HG_CHUNK, _head_cols(hd)])
        elif group == 6:
            hv = proj(6)
            hv_ref[...] = hv
            hvb_ref[...] = hv.astype(BF16)
        else:
            hg_ref[...] = _silu(proj(7))

    def scan():
        for p in range(N_HEADS // HEADS_PER_SCAN):
            run = jnp.zeros((SUBLANES, HEAD_DIM), F32)
            for t in range(HG_CHUNK):
                rows = pl.ds(t, SUBLANES, stride=SCAN_PITCH)
                run = run + a_ref[p, rows, :]
                b_ref[p, rows, :] = run

    def hg_factors(c, hd):
        cols = _head_cols(hd)
        p = hd // HEADS_PER_SCAN
        base = _scan_base(hd, c)
        b_row = lambda r: b_ref[p, base + r:base + r + 1, :]
        for r0 in range(0, HG_CHUNK, FACTOR_ROWS):
            rows = slice(c * HG_CHUNK + r0, c * HG_CHUNK + r0 + FACTOR_ROWS)
            q = hq_ref[rows, cols]
            k = hk_ref[rows, cols]
            a = a_ref[p, base + r0:base + r0 + FACTOR_ROWS, :]
            b = b_ref[p, base + r0:base + r0 + FACTOR_ROWS, :]
            factors = _hg_level_factors(q, k, a, b, b_row, r0)
            factors.append((q * jnp.exp2(b)).astype(BF16))
            factors.append((k * jnp.exp2(b_row(HG_CHUNK - 1) - b)).astype(BF16))
            for i, f in enumerate(factors):
                xf_ref[i, rows, cols] = f
        rows = slice(c * HG_CHUNK, (c + 1) * HG_CHUNK)
        for i, src in enumerate(list(range(len(HG_LEVELS))) + [HG_FACTORS - 1]):
            xt_ref[i, c * N_HEADS + hd] = xf_ref[src, rows, cols].T

    def hg_scores(c):
        for hd in range(N_HEADS):
            rows = slice(c * HG_CHUNK, (c + 1) * HG_CHUNK)
            cols = _head_cols(hd)
            s = jnp.zeros((HG_CHUNK, HG_CHUNK), BF16)
            for li in range(len(HG_LEVELS)):
                xl = xf_ref[li, rows, cols]
                s = s + _dot(xl, xt_ref[li, c * N_HEADS + hd]).astype(BF16) * lmask_ref[li]
            hs_ref[c * N_HEADS + hd] = s

    def hg_matmuls(c):
        out = []
        for hd in range(N_HEADS):
            rows = slice(c * HG_CHUNK, (c + 1) * HG_CHUNK)
            cols = _head_cols(hd)
            vb = hvb_ref[rows, cols]
            intra = _dot(hs_ref[c * N_HEADS + hd], vb)
            inter = _dot(xf_ref[HG_FACTORS - 2, rows, cols], hstate_ref[hd].astype(BF16))
            update = _dot(xt_ref[len(HG_LEVELS), c * N_HEADS + hd], vb)
            out.append((intra, inter, update))
        return out

    def hg_finish(c, results):
        for hd, (intra, inter, update) in enumerate(results):
            rows = slice(c * HG_CHUNK, (c + 1) * HG_CHUNK)
            cols = _head_cols(hd)
            last = _scan_base(hd, c) + HG_CHUNK - 1
            b_last = b_ref[hd // HEADS_PER_SCAN, last:last + 1, :]
            diag = jnp.sum(hq_ref[rows, cols] * hk_ref[rows, cols], axis=-1, keepdims=True)
            o = intra + diag * hv_ref[rows, cols] + inter
            decay = jnp.exp2(jnp.broadcast_to(b_last, (HEAD_DIM, HEAD_DIM)).T)
            hstate_ref[hd] = hstate_ref[hd] * decay + update
            merged_ref[rows, GROUP_W + hd * HEAD_DIM:GROUP_W + (hd + 1) * HEAD_DIM] = (
                _head_norm(o, hnw_ref[:, cols]) * hg_ref[rows, cols]).astype(BF16)

    def ret_scores(c):
        for hd in range(N_HEADS):
            rows = slice(c * RET_CHUNK, (c + 1) * RET_CHUNK)
            cols = _head_cols(hd)
            rs_ref[c * N_HEADS + hd] = (_dot(rq_ref[rows, cols], rk_ref[hd, :, rows]) * rdm_ref[hd]).astype(BF16)

    def ret_matmuls(c):
        out = []
        for hd in range(N_HEADS):
            rows = slice(c * RET_CHUNK, (c + 1) * RET_CHUNK)
            cols = _head_cols(hd)
            vb = rv_ref[rows, cols]
            intra = _dot(rs_ref[c * N_HEADS + hd], vb)
            inter = _dot(rq_ref[rows, cols], rstate_ref[hd].astype(BF16))
            update = _dot(rkdec_ref[hd, :, rows], vb)
            out.append((intra, inter, update))
        return out

    def ret_finish(c, results):
        for hd, (intra, inter, update) in enumerate(results):
            rows = slice(c * RET_CHUNK, (c + 1) * RET_CHUNK)
            cols = _head_cols(hd)
            o = intra + inter * rqd_ref[rows, cols]
            rstate_ref[hd] = rstate_ref[hd] * rcd_ref[hd] + update
            merged_ref[rows, cols] = (_head_norm(o, rnw_ref[:, cols]) * rg_ref[rows, cols]).astype(BF16)

    hb_ref[...] = _rms_mod(x_ref[...], n1_ref[...], mod_ref[1:2, :], mod_ref[0:1, :]).astype(BF16)

    chunk_heads = [(c, hd) for c in range(HG_CHUNKS) for hd in range(N_HEADS)]
    project(GROUP_ORDER[0])
    project(GROUP_ORDER[1])
    scan()
    rest = GROUP_ORDER[2:]
    per_group = -(-len(chunk_heads) // len(rest))
    for i, group in enumerate(rest):
        project(group)
        for c, hd in chunk_heads[i * per_group:(i + 1) * per_group]:
            hg_factors(c, hd)

    ret_scores(0)
    hg_scores(0)
    ret_scores(1)
    hg_scores(1)
    r0 = ret_matmuls(0)
    h0 = hg_matmuls(0)
    ret_finish(0, r0)
    hg_finish(0, h0)
    hg_scores(2)
    r1 = ret_matmuls(1)
    h1 = hg_matmuls(1)
    ret_finish(1, r1)
    hg_finish(1, h1)
    hg_scores(3)
    hg_finish(2, hg_matmuls(2))
    hg_finish(3, hg_matmuls(3))

    o_ref[...] = x_ref[...] + mod_ref[2:3, :] * _dot(merged_ref[...], wout_ref[...])


def _mixer_layer(layer, x, mod_l, n1, win, wout, rnw, hnw, lb, cos2, sin2, rdm, rqd, rkd, rcd, lmask):
    bsz, seq, d = x.shape
    ts = SEQ_TILE
    grid = (bsz, seq // ts)
    x_spec = pl.BlockSpec((None, ts, d), lambda b, t: (b, t, 0))
    tab_spec = pl.BlockSpec((ts, HEAD_DIM), lambda b, t: (t, 0))
    group = lambda dtype: pltpu.VMEM((ts, GROUP_W), dtype)
    scan = pltpu.VMEM((N_HEADS // HEADS_PER_SCAN, SUBLANES * SCAN_PITCH, HEAD_DIM), F32)
    keys_t = pltpu.VMEM((N_HEADS, HEAD_DIM, ts), BF16)
    return pl.pallas_call(
        _mixer_kernel,
        grid=grid,
        in_specs=[
            x_spec,
            pl.BlockSpec((None, N_MOD, d), lambda b, t: (b, 0, 0)),
            _const_spec(n1.shape), _layer_spec(win, layer), _layer_spec(wout, layer),
            _const_spec(rnw.shape), _const_spec(hnw.shape), _const_spec(lb.shape),
            tab_spec, tab_spec,
            _const_spec(rdm.shape), _const_spec(rqd.shape), _const_spec(rkd.shape), _const_spec(rcd.shape),
            _const_spec(lmask.shape),
        ],
        out_specs=x_spec,
        out_shape=jax.ShapeDtypeStruct(x.shape, x.dtype),
        scratch_shapes=[
            pltpu.VMEM((ts, d), BF16),
            group(BF16), keys_t, keys_t, group(BF16), group(F32),
            group(F32), group(F32), group(F32), group(BF16), group(F32), scan, scan,
            pltpu.VMEM((HG_FACTORS, ts, GROUP_W), BF16),
            pltpu.VMEM((HG_CHUNKS * N_HEADS, HG_CHUNK, HG_CHUNK), BF16),
            pltpu.VMEM((ts // RET_CHUNK * N_HEADS, RET_CHUNK, RET_CHUNK), BF16),
            pltpu.VMEM((len(HG_LEVELS) + 1, HG_CHUNKS * N_HEADS, HEAD_DIM, HG_CHUNK), BF16),
            pltpu.VMEM((ts, 2 * GROUP_W), BF16),
            pltpu.VMEM((N_HEADS, HEAD_DIM, HEAD_DIM), F32),
            pltpu.VMEM((N_HEADS, HEAD_DIM, HEAD_DIM), F32),
        ],
        compiler_params=pltpu.CompilerParams(
            dimension_semantics=("arbitrary", "arbitrary"), vmem_limit_bytes=VMEM_LIMIT_BYTES),
        name="mixer",
    )(x, mod_l, n1, win, wout, rnw, hnw, lb, cos2, sin2, rdm, rqd, rkd, rcd, lmask)


def _ffn_kernel(*refs, final):
    if final:
        (x_ref, mod_ref, n2_ref, wg_ref, wv_ref, wd_ref, cw_ref, cb_ref, fw_ref,
         o_ref, a_ref, y_ref) = refs
    else:
        (x_ref, mod_ref, n2_ref, wg_ref, wv_ref, wd_ref, cw_ref, cb_ref,
         o_ref, a_ref, y_ref) = refs
    ts = x_ref.shape[0]

    @pl.when(pl.program_id(1) == 0)
    def _():
        a_ref[0:CARRY_ROWS, :] = jnp.zeros((CARRY_ROWS, D_FF), F32)

    @pl.when(pl.program_id(1) != 0)
    def _():
        a_ref[0:CARRY_ROWS, :] = a_ref[ts:ts + CARRY_ROWS, :]

    x = x_ref[...]
    mod = mod_ref[...]
    hb = _rms_mod(x, n2_ref[...], mod[4:5], mod[3:4]).astype(BF16)
    for j in range(D_FF // FF_BLOCK):
        cols = slice(j * FF_BLOCK, (j + 1) * FF_BLOCK)
        a_ref[CARRY_ROWS:CARRY_ROWS + ts, cols] = _dot(hb, wg_ref[:, cols])
        acc = cb_ref[:, cols] + a_ref[CARRY_ROWS - 2:CARRY_ROWS - 2 + ts, cols] * cw_ref[0:1, cols]
        acc = acc + a_ref[CARRY_ROWS - 1:CARRY_ROWS - 1 + ts, cols] * cw_ref[1:2, cols]
        acc = acc + a_ref[CARRY_ROWS:CARRY_ROWS + ts, cols] * cw_ref[2:3, cols]
        y_ref[:, cols] = (_silu(acc) * _dot(hb, wv_ref[:, cols])).astype(BF16)
    out = x + mod[5:6] * _dot(y_ref[...], wd_ref[...])
    if final:
        out = out * lax.rsqrt(jnp.mean(out * out, axis=-1, keepdims=True) + EPS) * fw_ref[...]
    o_ref[...] = out


def _ffn_layer(layer, x, mod_l, n2, wg, wv, wd, cw, cb, final_w=None):
    bsz, seq, d = x.shape
    ts = FFN_TILE
    final = final_w is not None
    x_spec = pl.BlockSpec((None, ts, d), lambda b, t: (b, t, 0))
    args = [x, mod_l, n2, wg, wv, wd, cw, cb] + ([final_w] if final else [])
    in_specs = [x_spec, pl.BlockSpec((None, N_MOD, d), lambda b, t: (b, 0, 0)), _const_spec(n2.shape)]
    in_specs += [_layer_spec(w, layer) for w in (wg, wv, wd)]
    in_specs += [_const_spec(a.shape) for a in args[6:]]
    return pl.pallas_call(
        functools.partial(_ffn_kernel, final=final),
        grid=(bsz, seq // ts),
        in_specs=in_specs,
        out_specs=x_spec,
        out_shape=jax.ShapeDtypeStruct(x.shape, x.dtype),
        scratch_shapes=[
            pltpu.VMEM((ts + 2 * CARRY_ROWS, D_FF), F32),
            pltpu.VMEM((ts, D_FF), BF16),
        ],
        compiler_params=pltpu.CompilerParams(
            dimension_semantics=("arbitrary", "arbitrary"), vmem_limit_bytes=VMEM_LIMIT_BYTES),
        name="ffn",
    )(*args)


def _retention_tables():
    log_gamma = jnp.log(1.0 - jnp.exp2(-5.0 - jnp.arange(N_HEADS, dtype=F32)))
    lg = log_gamma[:, None, None]
    pos = jnp.arange(RET_CHUNK, dtype=F32)
    rel = pos[:, None] - pos[None, :]
    decay_mask = jnp.where(rel[None] >= 0, jnp.exp(lg * jnp.maximum(rel, 0.0)[None]), 0.0)
    ones = jnp.ones((1, 1, HEAD_DIM), F32)
    q_decay = jnp.exp(lg * (pos + 1.0)[None, :, None]) * ones
    k_decay = jnp.exp(lg * (RET_CHUNK - 1.0 - pos)[None, :, None]) * ones
    chunk_decay = jnp.exp(lg * RET_CHUNK) * jnp.ones((1, HEAD_DIM, HEAD_DIM), F32)

    def tile_layout(t):
        t = jnp.transpose(t, (1, 0, 2)).reshape(RET_CHUNK, GROUP_W)
        return jnp.tile(t, (SEQ_TILE // RET_CHUNK, 1))

    k_decay_t = jnp.tile(k_decay[:, :, 0], (1, SEQ_TILE // RET_CHUNK))[:, None, :]
    return decay_mask, tile_layout(q_decay), k_decay_t, chunk_decay


def kernel(x, c, w_in, w_out, ret_norm_w, hgrn_norm_w, hgrn_lb_logits, norm1_w, norm2_w,
           ada_w, ada_b, w_gate, w_val, conv_w, conv_b, w_down, final_norm_w):
    depth = w_in.shape[0]
    seq = x.shape[1]
    assert x.shape[2] == D_MODEL and seq % SEQ_TILE == 0 and SEQ_TILE % RET_CHUNK == 0 and seq % FFN_TILE == 0
    assert HEADS_PER_SCAN * HG_CHUNKS == SUBLANES and N_HEADS % HEADS_PER_SCAN == 0

    mods = _ada_mods(c, ada_w, ada_b)
    cos2, sin2 = _rope_tables(seq)
    rdm, rqd, rkd, rcd = _retention_tables()
    lmask = jnp.asarray(_hg_level_masks(), BF16)
    p = jax.nn.softmax(hgrn_lb_logits.astype(F32), axis=0)
    lower_bounds = jnp.cumsum(p, axis=0) - p[0:1]

    win, wout, wg, wv, wd = (w.astype(BF16) for w in (w_in, w_out, w_gate, w_val, w_down))
    row = lambda v: v.reshape(1, -1)
    for l in range(depth):
        x = _mixer_layer(l, x, mods[l], row(norm1_w[l]), win, wout,
                         row(ret_norm_w[l]), row(hgrn_norm_w[l]), row(lower_bounds[l]),
                         cos2, sin2, rdm, rqd, rkd, rcd, lmask)
        x = _ffn_layer(l, x, mods[l], row(norm2_w[l]), wg, wv, wd, conv_w[l], row(conv_b[l]),
                       final_w=row(final_norm_w) if l == depth - 1 else None)
    return x
```

```python
import functools

import numpy as np
import jax
import jax.numpy as jnp
from jax import lax
from jax.experimental import pallas as pl
from jax.experimental.pallas import tpu as pltpu

F32 = jnp.float32
BF16 = jnp.bfloat16

SUBLANES = 8
D_MODEL = 1024
HEAD_DIM = 128
N_HEADS = 4
GROUP_W = N_HEADS * HEAD_DIM
IN_COLS = 8 * GROUP_W
D_FF = 2816
CONV_WIDTH = 3
ROPE_BASE = 10000.0
EPS = 1e-6
N_MOD = 6

SEQ_TILE = 512
FFN_TILE = 1024
RET_CHUNK = 256
HG_CHUNK = 128
HG_LEVELS = (64, 32, 16, 8, 4, 2, 1)
HG_FACTORS = len(HG_LEVELS) + 2
HG_CHUNKS = SEQ_TILE // HG_CHUNK
FACTOR_ROWS = 16
HEADS_PER_SCAN = SUBLANES // HG_CHUNKS
SCAN_PITCH = HG_CHUNK + SUBLANES
GROUP_ORDER = (5, 4, 6, 7, 0, 1, 2, 3)
FF_BLOCK = 256
CARRY_ROWS = 8
ADA_BLOCK = 1536
ROPE_FINE = 128
VMEM_LIMIT_BYTES = 56 * 1024 * 1024


def _const_spec(shape):
    nd = len(shape)
    return pl.BlockSpec(shape, lambda *_: (0,) * nd, pipeline_mode=pl.Buffered(1))


def _layer_spec(stacked, layer):
    nd = stacked.ndim - 1
    return pl.BlockSpec((None,) + stacked.shape[1:], lambda *_: (layer,) + (0,) * nd,
                        pipeline_mode=pl.Buffered(1))


def _sigmoid(z):
    return 1.0 / (1.0 + jnp.exp(-z))


def _silu(z):
    return z * _sigmoid(z)


def _dot(a, b):
    return jnp.dot(a, b, preferred_element_type=F32)


def _dot_nt(a, b):
    return lax.dot_general(a, b, (((1,), (1,)), ((), ())), preferred_element_type=F32)


def _dot_tn(a, b):
    return lax.dot_general(a, b, (((0,), (0,)), ((), ())), preferred_element_type=F32)


def _ada_kernel(c_ref, w_ref, b_ref, o_ref):
    c = c_ref[...]
    o_ref[...] = jnp.dot(_silu(c), w_ref[...], preferred_element_type=F32,
                         precision=lax.Precision.HIGHEST) + b_ref[...]


def _ada_mods(c, ada_w, ada_b):
    depth, d, n = ada_w.shape
    b = c.shape[0]
    rows = SUBLANES
    c_pad = jnp.zeros((rows, d), F32).at[:b].set(c)
    out = pl.pallas_call(
        _ada_kernel,
        grid=(depth, n // ADA_BLOCK),
        in_specs=[
            pl.BlockSpec((rows, d), lambda l, j: (0, 0)),
            pl.BlockSpec((None, d, ADA_BLOCK), lambda l, j: (l, 0, j)),
            pl.BlockSpec((None, 1, ADA_BLOCK), lambda l, j: (l, 0, j)),
        ],
        out_specs=pl.BlockSpec((None, rows, ADA_BLOCK), lambda l, j: (l, 0, j)),
        out_shape=jax.ShapeDtypeStruct((depth, rows, n), F32),
        name="ada_mod",
    )(c_pad, ada_w, ada_b.reshape(depth, 1, n))
    return out[:, :b].reshape(depth, b, N_MOD, d)


def _rope_kernel(f_ref, cos_ref, sin_ref):
    rows = cos_ref.shape[0]
    n_coarse = rows // ROPE_FINE
    f = f_ref[...]
    fine = lax.broadcasted_iota(jnp.int32, (ROPE_FINE, HEAD_DIM), 0).astype(F32) * f
    coarse = (lax.broadcasted_iota(jnp.int32, (n_coarse, HEAD_DIM), 0) * ROPE_FINE
              + pl.program_id(0) * rows).astype(F32) * f
    cos_f, sin_f = jnp.cos(fine), jnp.sin(fine)
    cos_c, sin_c = jnp.cos(coarse), jnp.sin(coarse)
    lane = lax.broadcasted_iota(jnp.int32, (ROPE_FINE, HEAD_DIM), 1)
    for i in range(n_coarse):
        cc, sc = cos_c[i:i + 1, :], sin_c[i:i + 1, :]
        sin = sc * cos_f + cc * sin_f
        cos_ref[i * ROPE_FINE:(i + 1) * ROPE_FINE, :] = cc * cos_f - sc * sin_f
        sin_ref[i * ROPE_FINE:(i + 1) * ROPE_FINE, :] = jnp.where(lane < HEAD_DIM // 2, -sin, sin)


def _rope_tables(seq):
    inv_freq = ROPE_BASE ** (-jnp.arange(0, HEAD_DIM, 2, dtype=F32) / HEAD_DIM)
    f2 = jnp.concatenate([inv_freq, inv_freq]).reshape(1, HEAD_DIM)
    rows = 2048
    return pl.pallas_call(
        _rope_kernel,
        grid=(seq // rows,),
        in_specs=[pl.BlockSpec((1, HEAD_DIM), lambda i: (0, 0))],
        out_specs=[pl.BlockSpec((rows, HEAD_DIM), lambda i: (i, 0))] * 2,
        out_shape=[jax.ShapeDtypeStruct((seq, HEAD_DIM), F32)] * 2,
        name="rope_tables",
    )(f2)


def _hg_level_masks():
    c = HG_CHUNK
    i = np.arange(c)[:, None]
    j = np.arange(c)[None, :]
    out = []
    for m in HG_LEVELS:
        blk = 2 * m
        out.append(((i // blk == j // blk) & (i % blk >= m) & (j % blk < m)).astype(np.float32))
    return np.stack(out)


def _rms_mod(x, w, scale, shift):
    y = x * lax.rsqrt(jnp.mean(x * x, axis=-1, keepdims=True) + EPS)
    return y * (w * (1.0 + scale)) + shift


def _head_norm(o, w):
    return o * lax.rsqrt(jnp.mean(o * o, axis=-1, keepdims=True) + EPS) * w


def _head_cols(hd):
    return slice(hd * HEAD_DIM, (hd + 1) * HEAD_DIM)


def _scan_base(hd, c):
    return ((hd % HEADS_PER_SCAN) * HG_CHUNKS + c) * SCAN_PITCH


def _hg_level_factors(q, k, a, b, b_row, r0):
    n = q.shape[0]
    sub = lax.broadcasted_iota(jnp.int32, q.shape, 0) % SUBLANES
    out = []
    for m in HG_LEVELS:
        blk = 2 * m
        if m >= SUBLANES:
            parts = []
            for v in range(0, n, SUBLANES):
                sl = slice(v, v + SUBLANES)
                ref = b_row((r0 + v) // blk * blk + m - 1)
                if (r0 + v) % blk >= m:
                    parts.append(q[sl] * jnp.exp2(b[sl] - ref))
                else:
                    parts.append(k[sl] * jnp.exp2(ref - b[sl]))
            x = jnp.concatenate(parts, axis=0)
        elif m == 1:
            x = jnp.where(sub % 2 == 1, q * jnp.exp2(a), k)
        else:
            refs = []
            for v in range(0, n, SUBLANES):
                r = None
                for off in range(m - 1, SUBLANES, blk):
                    row = jnp.broadcast_to(b_row(r0 + v + off), (SUBLANES, HEAD_DIM))
                    r = row if r is None else jnp.where(sub[:SUBLANES] < off - m + 1, r, row)
                refs.append(r)
            e = b - jnp.concatenate(refs, axis=0)
            x = jnp.where(sub % blk >= m, q, k) * jnp.exp2(jnp.minimum(e, -e))
        out.append(x.astype(BF16))
    return out


def _mixer_kernel(x_ref, mod_ref, n1_ref, win_ref, wout_ref, rnw_ref, hnw_ref, lb_ref,
                  cos_ref, sin_ref, rdm_ref, rqd_ref, rkd_ref, rcd_ref, lmask_ref,
                  o_ref,
                  hb_ref, rq_ref, rk_ref, rkdec_ref, rv_ref, rg_ref,
                  hq_ref, hk_ref, hv_ref, hvb_ref, hg_ref, a_ref, b_ref, xf_ref,
                  hs_ref, rs_ref, xt_ref, merged_ref, rstate_ref, hstate_ref):
    ts = x_ref.shape[0]

    @pl.when(pl.program_id(1) == 0)
    def _():
        rstate_ref[...] = jnp.zeros_like(rstate_ref)
        hstate_ref[...] = jnp.zeros_like(hstate_ref)

    def proj(group):
        return _dot(hb_ref[...], win_ref[:, group * GROUP_W:(group + 1) * GROUP_W])

    def rotary(t):
        parts = []
        for hd in range(N_HEADS):
            th = t[:, _head_cols(hd)]
            parts.append(th * cos_ref[...] + pltpu.roll(th, HEAD_DIM // 2, 1) * sin_ref[...])
        return jnp.concatenate(parts, axis=1)

    def project(group):
        acc = proj(group)
        if group == 0:
            rq_ref[...] = rotary(acc).astype(BF16)
        elif group == 1:
            rk = rotary(acc) * (HEAD_DIM ** -0.5)
            for hd in range(N_HEADS):
                rkt = rk[:, _head_cols(hd)].T
                rk_ref[hd] = rkt.astype(BF16)
                rkdec_ref[hd] = (rkt * rkd_ref[hd]).astype(BF16)
        elif group == 2:
            rv_ref[...] = acc.astype(BF16)
        elif group == 3:
            rg_ref[...] = _silu(acc)
        elif group == 4:
            hq_ref[...] = _silu(acc)
        elif group == 5:
            z = acc
            lb = lb_ref[...]
            e = jnp.exp(-jnp.abs(z))
            r = 1.0 / (1.0 + e)
            er = e * r
            a2 = jnp.log2(lb + (1.0 - lb) * jnp.where(z >= 0, r, er))
            hk_ref[...] = (1.0 - lb) * jnp.where(z >= 0, er, r)
            for hd in range(N_HEADS):
                for c in range(HG_CHUNKS):
                    a_ref[hd // HEADS_PER_SCAN, pl.ds(_scan_base(hd, c), HG_CHUNK), :] = (
                        a2[c * HG_CHUNK:(c + 1) * HG_CHUNK, _head_cols(hd)])
        elif group == 6:
            hv = acc
            hv_ref[...] = hv
            hvb_ref[...] = hv.astype(BF16)
        else:
            hg_ref[...] = _silu(acc)

    def scan():
        for p in range(N_HEADS // HEADS_PER_SCAN):
            run = jnp.zeros((SUBLANES, HEAD_DIM), F32)
            for t in range(HG_CHUNK):
                rows = pl.ds(t, SUBLANES, stride=SCAN_PITCH)
                run = run + a_ref[p, rows, :]
                b_ref[p, rows, :] = run

    def hg_factors(c, hd):
        cols = _head_cols(hd)
        p = hd // HEADS_PER_SCAN
        base = _scan_base(hd, c)
        b_row = lambda r: b_ref[p, base + r:base + r + 1, :]
        for r0 in range(0, HG_CHUNK, FACTOR_ROWS):
            rows = slice(c * HG_CHUNK + r0, c * HG_CHUNK + r0 + FACTOR_ROWS)
            q = hq_ref[rows, cols]
            k = hk_ref[rows, cols]
            a = a_ref[p, base + r0:base + r0 + FACTOR_ROWS, :]
            b = b_ref[p, base + r0:base + r0 + FACTOR_ROWS, :]
            factors = _hg_level_factors(q, k, a, b, b_row, r0)
            factors.append((q * jnp.exp2(b)).astype(BF16))
            factors.append((k * jnp.exp2(b_row(HG_CHUNK - 1) - b)).astype(BF16))
            for i, f in enumerate(factors):
                xf_ref[i, rows, cols] = f
        rows = slice(c * HG_CHUNK, (c + 1) * HG_CHUNK)
        for i, src in enumerate(list(range(len(HG_LEVELS))) + [HG_FACTORS - 1]):
            xt_ref[i, c * N_HEADS + hd] = xf_ref[src, rows, cols].T

    def hg_scores(c):
        for hd in range(N_HEADS):
            rows = slice(c * HG_CHUNK, (c + 1) * HG_CHUNK)
            cols = _head_cols(hd)
            s = jnp.zeros((HG_CHUNK, HG_CHUNK), BF16)
            for li in range(len(HG_LEVELS)):
                xl = xf_ref[li, rows, cols]
                s = s + _dot(xl, xt_ref[li, c * N_HEADS + hd]).astype(BF16) * lmask_ref[li]
            hs_ref[c * N_HEADS + hd] = s

    def hg_matmuls(c):
        out = []
        for hd in range(N_HEADS):
            rows = slice(c * HG_CHUNK, (c + 1) * HG_CHUNK)
            cols = _head_cols(hd)
            vb = hvb_ref[rows, cols]
            intra = _dot(hs_ref[c * N_HEADS + hd], vb)
            inter = _dot(xf_ref[HG_FACTORS - 2, rows, cols], hstate_ref[hd].astype(BF16))
            update = _dot(xt_ref[len(HG_LEVELS), c * N_HEADS + hd], vb)
            out.append((intra, inter, update))
        return out

    def hg_finish(c, results):
        for hd, (intra, inter, update) in enumerate(results):
            rows = slice(c * HG_CHUNK, (c + 1) * HG_CHUNK)
            cols = _head_cols(hd)
            last = _scan_base(hd, c) + HG_CHUNK - 1
            b_last = b_ref[hd // HEADS_PER_SCAN, last:last + 1, :]
            diag = jnp.sum(hq_ref[rows, cols] * hk_ref[rows, cols], axis=-1, keepdims=True)
            o = intra + diag * hv_ref[rows, cols] + inter
            decay = jnp.exp2(jnp.broadcast_to(b_last, (HEAD_DIM, HEAD_DIM)).T)
            hstate_ref[hd] = hstate_ref[hd] * decay + update
            merged_ref[rows, GROUP_W + hd * HEAD_DIM:GROUP_W + (hd + 1) * HEAD_DIM] = (
                _head_norm(o, hnw_ref[:, cols]) * hg_ref[rows, cols]).astype(BF16)

    def ret_scores(c):
        for hd in range(N_HEADS):
            rows = slice(c * RET_CHUNK, (c + 1) * RET_CHUNK)
            cols = _head_cols(hd)
            rs_ref[c * N_HEADS + hd] = (_dot(rq_ref[rows, cols], rk_ref[hd, :, rows]) * rdm_ref[hd]).astype(BF16)

    def ret_matmuls(c):
        out = []
        for hd in range(N_HEADS):
            rows = slice(c * RET_CHUNK, (c + 1) * RET_CHUNK)
            cols = _head_cols(hd)
            vb = rv_ref[rows, cols]
            intra = _dot(rs_ref[c * N_HEADS + hd], vb)
            inter = _dot(rq_ref[rows, cols], rstate_ref[hd].astype(BF16))
            update = _dot(rkdec_ref[hd, :, rows], vb)
            out.append((intra, inter, update))
        return out

    def ret_finish(c, results):
        for hd, (intra, inter, update) in enumerate(results):
            rows = slice(c * RET_CHUNK, (c + 1) * RET_CHUNK)
            cols = _head_cols(hd)
            o = intra + inter * rqd_ref[rows, cols]
            rstate_ref[hd] = rstate_ref[hd] * rcd_ref[hd] + update
            merged_ref[rows, cols] = (_head_norm(o, rnw_ref[:, cols]) * rg_ref[rows, cols]).astype(BF16)

    hb_ref[...] = _rms_mod(x_ref[...], n1_ref[...], mod_ref[1:2, :], mod_ref[0:1, :]).astype(BF16)

    chunk_heads = [(c, hd) for c in range(HG_CHUNKS) for hd in range(N_HEADS)]
    project(GROUP_ORDER[0])
    project(GROUP_ORDER[1])
    scan()
    rest = GROUP_ORDER[2:]
    per_group = -(-len(chunk_heads) // len(rest))
    for i, group in enumerate(rest):
        project(group)
        for c, hd in chunk_heads[i * per_group:(i + 1) * per_group]:
            hg_factors(c, hd)

    ret_scores(0)
    hg_scores(0)
    ret_scores(1)
    hg_scores(1)
    r0 = ret_matmuls(0)
    h0 = hg_matmuls(0)
    ret_finish(0, r0)
    hg_finish(0, h0)
    hg_scores(2)
    r1 = ret_matmuls(1)
    h1 = hg_matmuls(1)
    ret_finish(1, r1)
    hg_finish(1, h1)
    hg_scores(3)
    hg_finish(2, hg_matmuls(2))
    hg_finish(3, hg_matmuls(3))

    o_ref[...] = x_ref[...] + mod_ref[2:3, :] * _dot(merged_ref[...], wout_ref[...])


def _mixer_layer(layer, x, mod_l, n1, win, wout, rnw, hnw, lb, cos2, sin2, rdm, rqd, rkd, rcd, lmask):
    bsz, seq, d = x.shape
    ts = SEQ_TILE
    grid = (bsz, seq // ts)
    x_spec = pl.BlockSpec((None, ts, d), lambda b, t: (b, t, 0))
    tab_spec = pl.BlockSpec((ts, HEAD_DIM), lambda b, t: (t, 0))
    group = lambda dtype: pltpu.VMEM((ts, GROUP_W), dtype)
    scan = pltpu.VMEM((N_HEADS // HEADS_PER_SCAN, SUBLANES * SCAN_PITCH, HEAD_DIM), F32)
    keys_t = pltpu.VMEM((N_HEADS, HEAD_DIM, ts), BF16)
    return pl.pallas_call(
        _mixer_kernel,
        grid=grid,
        in_specs=[
            x_spec,
            pl.BlockSpec((None, N_MOD, d), lambda b, t: (b, 0, 0)),
            _const_spec(n1.shape), _layer_spec(win, layer), _layer_spec(wout, layer),
            _const_spec(rnw.shape), _const_spec(hnw.shape), _const_spec(lb.shape),
            tab_spec, tab_spec,
            _const_spec(rdm.shape), _const_spec(rqd.shape), _const_spec(rkd.shape), _const_spec(rcd.shape),
            _const_spec(lmask.shape),
        ],
        out_specs=x_spec,
        out_shape=jax.ShapeDtypeStruct(x.shape, x.dtype),
        scratch_shapes=[
            pltpu.VMEM((ts, d), BF16),
            group(BF16), keys_t, keys_t, group(BF16), group(F32),
            group(F32), group(F32), group(F32), group(BF16), group(F32), scan, scan,
            pltpu.VMEM((HG_FACTORS, ts, GROUP_W), BF16),
            pltpu.VMEM((HG_CHUNKS * N_HEADS, HG_CHUNK, HG_CHUNK), BF16),
            pltpu.VMEM((ts // RET_CHUNK * N_HEADS, RET_CHUNK, RET_CHUNK), BF16),
            pltpu.VMEM((len(HG_LEVELS) + 1, HG_CHUNKS * N_HEADS, HEAD_DIM, HG_CHUNK), BF16),
            pltpu.VMEM((ts, 2 * GROUP_W), BF16),
            pltpu.VMEM((N_HEADS, HEAD_DIM, HEAD_DIM), F32),
            pltpu.VMEM((N_HEADS, HEAD_DIM, HEAD_DIM), F32),
        ],
        compiler_params=pltpu.CompilerParams(
            dimension_semantics=("arbitrary", "arbitrary"), vmem_limit_bytes=VMEM_LIMIT_BYTES),
        name="mixer",
    )(x, mod_l, n1, win, wout, rnw, hnw, lb, cos2, sin2, rdm, rqd, rkd, rcd, lmask)


def _ffn_kernel(*refs, final):
    if final:
        (x_ref, mod_ref, n2_ref, wg_ref, wv_ref, wd_ref, cw_ref, cb_ref, fw_ref,
         o_ref, a_ref, y_ref) = refs
    else:
        (x_ref, mod_ref, n2_ref, wg_ref, wv_ref, wd_ref, cw_ref, cb_ref,
         o_ref, a_ref, y_ref) = refs
    ts = x_ref.shape[0]

    @pl.when(pl.program_id(1) == 0)
    def _():
        a_ref[0:CARRY_ROWS, :] = jnp.zeros((CARRY_ROWS, D_FF), F32)

    @pl.when(pl.program_id(1) != 0)
    def _():
        a_ref[0:CARRY_ROWS, :] = a_ref[ts:ts + CARRY_ROWS, :]

    x = x_ref[...]
    mod = mod_ref[...]
    hb = _rms_mod(x, n2_ref[...], mod[4:5], mod[3:4]).astype(BF16)
    for j in range(D_FF // FF_BLOCK):
        cols = slice(j * FF_BLOCK, (j + 1) * FF_BLOCK)
        a_ref[CARRY_ROWS:CARRY_ROWS + ts, cols] = _dot(hb, wg_ref[:, cols])
        acc = cb_ref[:, cols] + a_ref[CARRY_ROWS - 2:CARRY_ROWS - 2 + ts, cols] * cw_ref[0:1, cols]
        acc = acc + a_ref[CARRY_ROWS - 1:CARRY_ROWS - 1 + ts, cols] * cw_ref[1:2, cols]
        acc = acc + a_ref[CARRY_ROWS:CARRY_ROWS + ts, cols] * cw_ref[2:3, cols]
        y_ref[:, cols] = (_silu(acc) * _dot(hb, wv_ref[:, cols])).astype(BF16)
    out = x + mod[5:6] * _dot(y_ref[...], wd_ref[...])
    if final:
        out = out * lax.rsqrt(jnp.mean(out * out, axis=-1, keepdims=True) + EPS) * fw_ref[...]
    o_ref[...] = out


def _ffn_layer(layer, x, mod_l, n2, wg, wv, wd, cw, cb, final_w=None):
    bsz, seq, d = x.shape
    ts = FFN_TILE
    final = final_w is not None
    x_spec = pl.BlockSpec((None, ts, d), lambda b, t: (b, t, 0))
    args = [x, mod_l, n2, wg, wv, wd, cw, cb] + ([final_w] if final else [])
    in_specs = [x_spec, pl.BlockSpec((None, N_MOD, d), lambda b, t: (b, 0, 0)), _const_spec(n2.shape)]
    in_specs += [_layer_spec(w, layer) for w in (wg, wv, wd)]
    in_specs += [_const_spec(a.shape) for a in args[6:]]
    return pl.pallas_call(
        functools.partial(_ffn_kernel, final=final),
        grid=(bsz, seq // ts),
        in_specs=in_specs,
        out_specs=x_spec,
        out_shape=jax.ShapeDtypeStruct(x.shape, x.dtype),
        scratch_shapes=[
            pltpu.VMEM((ts + 2 * CARRY_ROWS, D_FF), F32),
            pltpu.VMEM((ts, D_FF), BF16),
        ],
        compiler_params=pltpu.CompilerParams(
            dimension_semantics=("arbitrary", "arbitrary"), vmem_limit_bytes=VMEM_LIMIT_BYTES),
        name="ffn",
    )(*args)


def _retention_tables():
    log_gamma = jnp.log(1.0 - jnp.exp2(-5.0 - jnp.arange(N_HEADS, dtype=F32)))
    lg = log_gamma[:, None, None]
    pos = jnp.arange(RET_CHUNK, dtype=F32)
    rel = pos[:, None] - pos[None, :]
    decay_mask = jnp.where(rel[None] >= 0, jnp.exp(lg * jnp.maximum(rel, 0.0)[None]), 0.0)
    ones = jnp.ones((1, 1, HEAD_DIM), F32)
    q_decay = jnp.exp(lg * (pos + 1.0)[None, :, None]) * ones
    k_decay = jnp.exp(lg * (RET_CHUNK - 1.0 - pos)[None, :, None]) * ones
    chunk_decay = jnp.exp(lg * RET_CHUNK) * jnp.ones((1, HEAD_DIM, HEAD_DIM), F32)

    def tile_layout(t):
        t = jnp.transpose(t, (1, 0, 2)).reshape(RET_CHUNK, GROUP_W)
        return jnp.tile(t, (SEQ_TILE // RET_CHUNK, 1))

    k_decay_t = jnp.tile(k_decay[:, :, 0], (1, SEQ_TILE // RET_CHUNK))[:, None, :]
    return decay_mask, tile_layout(q_decay), k_decay_t, chunk_decay


def kernel(x, c, w_in, w_out, ret_norm_w, hgrn_norm_w, hgrn_lb_logits, norm1_w, norm2_w,
           ada_w, ada_b, w_gate, w_val, conv_w, conv_b, w_down, final_norm_w):
    depth = w_in.shape[0]
    seq = x.shape[1]
    assert x.shape[2] == D_MODEL and seq % SEQ_TILE == 0 and SEQ_TILE % RET_CHUNK == 0 and seq % FFN_TILE == 0
    assert HEADS_PER_SCAN * HG_CHUNKS == SUBLANES and N_HEADS % HEADS_PER_SCAN == 0

    mods = _ada_mods(c, ada_w, ada_b)
    cos2, sin2 = _rope_tables(seq)
    rdm, rqd, rkd, rcd = _retention_tables()
    lmask = jnp.asarray(_hg_level_masks(), BF16)
    p = jax.nn.softmax(hgrn_lb_logits.astype(F32), axis=0)
    lower_bounds = jnp.cumsum(p, axis=0) - p[0:1]

    win, wout, wg, wv, wd = (w.astype(BF16) for w in (w_in, w_out, w_gate, w_val, w_down))
    row = lambda v: v.reshape(1, -1)
    for l in range(depth):
        x = _mixer_layer(l, x, mods[l], row(norm1_w[l]), win, wout,
                         row(ret_norm_w[l]), row(hgrn_norm_w[l]), row(lower_bounds[l]),
                         cos2, sin2, rdm, rqd, rkd, rcd, lmask)
        x = _ffn_layer(l, x, mods[l], row(norm2_w[l]), wg, wv, wd, conv_w[l], row(conv_b[l]),
                       final_w=row(final_norm_w) if l == depth - 1 else None)
    return x
```

```python
import functools

import numpy as np
import jax
import jax.numpy as jnp
from jax import lax
from jax.experimental import pallas as pl
from jax.experimental.pallas import tpu as pltpu

F32 = jnp.float32
BF16 = jnp.bfloat16

SUBLANES = 8
D_MODEL = 1024
HEAD_DIM = 128
N_HEADS = 4
GROUP_W = N_HEADS * HEAD_DIM
IN_COLS = 8 * GROUP_W
D_FF = 2816
CONV_WIDTH = 3
ROPE_BASE = 10000.0
EPS = 1e-6
N_MOD = 6

SEQ_TILE = 512
FFN_TILE = 1024
RET_CHUNK = 128
HG_CHUNK = 128
HG_LEVELS = (64, 32, 16, 8, 4, 2, 1)
HG_FACTORS = len(HG_LEVELS) + 2
HG_CHUNKS = SEQ_TILE // HG_CHUNK
FACTOR_ROWS = 16
HEADS_PER_SCAN = SUBLANES // HG_CHUNKS
SCAN_PITCH = HG_CHUNK + SUBLANES
GROUP_ORDER = (5, 4, 6, 7, 0, 1, 2, 3)
FF_BLOCK = 256
CARRY_ROWS = 8
ADA_BLOCK = 1536
ROPE_FINE = 128
VMEM_LIMIT_BYTES = 56 * 1024 * 1024


def _const_spec(shape):
    nd = len(shape)
    return pl.BlockSpec(shape, lambda *_: (0,) * nd, pipeline_mode=pl.Buffered(1))


def _layer_spec(stacked, layer):
    nd = stacked.ndim - 1
    return pl.BlockSpec((None,) + stacked.shape[1:], lambda *_: (layer,) + (0,) * nd,
                        pipeline_mode=pl.Buffered(1))


def _sigmoid(z):
    return 1.0 / (1.0 + jnp.exp(-z))


def _silu(z):
    return z * _sigmoid(z)


def _dot(a, b):
    return jnp.dot(a, b, preferred_element_type=F32)


def _dot_nt(a, b):
    return lax.dot_general(a, b, (((1,), (1,)), ((), ())), preferred_element_type=F32)


def _dot_tn(a, b):
    return lax.dot_general(a, b, (((0,), (0,)), ((), ())), preferred_element_type=F32)


def _ada_kernel(c_ref, w_ref, b_ref, o_ref):
    c = c_ref[...]
    o_ref[...] = jnp.dot(_silu(c), w_ref[...], preferred_element_type=F32,
                         precision=lax.Precision.HIGHEST) + b_ref[...]


def _ada_mods(c, ada_w, ada_b):
    depth, d, n = ada_w.shape
    b = c.shape[0]
    rows = SUBLANES
    c_pad = jnp.zeros((rows, d), F32).at[:b].set(c)
    out = pl.pallas_call(
        _ada_kernel,
        grid=(depth, n // ADA_BLOCK),
        in_specs=[
            pl.BlockSpec((rows, d), lambda l, j: (0, 0)),
            pl.BlockSpec((None, d, ADA_BLOCK), lambda l, j: (l, 0, j)),
            pl.BlockSpec((None, 1, ADA_BLOCK), lambda l, j: (l, 0, j)),
        ],
        out_specs=pl.BlockSpec((None, rows, ADA_BLOCK), lambda l, j: (l, 0, j)),
        out_shape=jax.ShapeDtypeStruct((depth, rows, n), F32),
        name="ada_mod",
    )(c_pad, ada_w, ada_b.reshape(depth, 1, n))
    return out[:, :b].reshape(depth, b, N_MOD, d)


def _rope_kernel(f_ref, cos_ref, sin_ref):
    rows = cos_ref.shape[0]
    n_coarse = rows // ROPE_FINE
    f = f_ref[...]
    fine = lax.broadcasted_iota(jnp.int32, (ROPE_FINE, HEAD_DIM), 0).astype(F32) * f
    coarse = (lax.broadcasted_iota(jnp.int32, (n_coarse, HEAD_DIM), 0) * ROPE_FINE
              + pl.program_id(0) * rows).astype(F32) * f
    cos_f, sin_f = jnp.cos(fine), jnp.sin(fine)
    cos_c, sin_c = jnp.cos(coarse), jnp.sin(coarse)
    lane = lax.broadcasted_iota(jnp.int32, (ROPE_FINE, HEAD_DIM), 1)
    for i in range(n_coarse):
        cc, sc = cos_c[i:i + 1, :], sin_c[i:i + 1, :]
        sin = sc * cos_f + cc * sin_f
        cos_ref[i * ROPE_FINE:(i + 1) * ROPE_FINE, :] = cc * cos_f - sc * sin_f
        sin_ref[i * ROPE_FINE:(i + 1) * ROPE_FINE, :] = jnp.where(lane < HEAD_DIM // 2, -sin, sin)


def _rope_tables(seq):
    inv_freq = ROPE_BASE ** (-jnp.arange(0, HEAD_DIM, 2, dtype=F32) / HEAD_DIM)
    f2 = jnp.concatenate([inv_freq, inv_freq]).reshape(1, HEAD_DIM)
    rows = 2048
    return pl.pallas_call(
        _rope_kernel,
        grid=(seq // rows,),
        in_specs=[pl.BlockSpec((1, HEAD_DIM), lambda i: (0, 0))],
        out_specs=[pl.BlockSpec((rows, HEAD_DIM), lambda i: (i, 0))] * 2,
        out_shape=[jax.ShapeDtypeStruct((seq, HEAD_DIM), F32)] * 2,
        name="rope_tables",
    )(f2)


def _hg_level_masks():
    c = HG_CHUNK
    i = np.arange(c)[:, None]
    j = np.arange(c)[None, :]
    out = []
    for m in HG_LEVELS:
        blk = 2 * m
        out.append(((i // blk == j // blk) & (i % blk >= m) & (j % blk < m)).astype(np.float32))
    return np.stack(out)


def _rms_mod(x, w, scale, shift):
    y = x * lax.rsqrt(jnp.mean(x * x, axis=-1, keepdims=True) + EPS)
    return y * (w * (1.0 + scale)) + shift


def _head_norm(o, w):
    return o * lax.rsqrt(jnp.mean(o * o, axis=-1, keepdims=True) + EPS) * w


def _head_cols(hd):
    return slice(hd * HEAD_DIM, (hd + 1) * HEAD_DIM)


def _scan_base(hd, c):
    return ((hd % HEADS_PER_SCAN) * HG_CHUNKS + c) * SCAN_PITCH


def _hg_level_factors(q, k, a, b, b_row, r0):
    n = q.shape[0]
    sub = lax.broadcasted_iota(jnp.int32, q.shape, 0) % SUBLANES
    out = []
    for m in HG_LEVELS:
        blk = 2 * m
        if m >= SUBLANES:
            parts = []
            for v in range(0, n, SUBLANES):
                sl = slice(v, v + SUBLANES)
                ref = b_row((r0 + v) // blk * blk + m - 1)
                if (r0 + v) % blk >= m:
                    parts.append(q[sl] * jnp.exp2(b[sl] - ref))
                else:
                    parts.append(k[sl] * jnp.exp2(ref - b[sl]))
            x = jnp.concatenate(parts, axis=0)
        elif m == 1:
            x = jnp.where(sub % 2 == 1, q * jnp.exp2(a), k)
        else:
            refs = []
            for v in range(0, n, SUBLANES):
                r = None
                for off in range(m - 1, SUBLANES, blk):
                    row = jnp.broadcast_to(b_row(r0 + v + off), (SUBLANES, HEAD_DIM))
                    r = row if r is None else jnp.where(sub[:SUBLANES] < off - m + 1, r, row)
                refs.append(r)
            e = b - jnp.concatenate(refs, axis=0)
            x = jnp.where(sub % blk >= m, q, k) * jnp.exp2(jnp.minimum(e, -e))
        out.append(x.astype(BF16))
    return out


def _mixer_kernel(x_ref, mod_ref, n1_ref, win_ref, wout_ref, rnw_ref, hnw_ref, lb_ref,
                  cos_ref, sin_ref, rdm_ref, rqd_ref, rkd_ref, rcd_ref, lmask_ref,
                  o_ref,
                  hb_ref, rq_ref, rk_ref, rkdec_ref, rv_ref, rg_ref,
                  hq_ref, hk_ref, hv_ref, hvb_ref, hg_ref, a_ref, b_ref, xf_ref,
                  hs_ref, rs_ref, xt_ref, merged_ref, rstate_ref, hstate_ref):
    ts = x_ref.shape[0]

    @pl.when(pl.program_id(1) == 0)
    def _():
        rstate_ref[...] = jnp.zeros_like(rstate_ref)
        hstate_ref[...] = jnp.zeros_like(hstate_ref)

    def proj(group):
        return _dot(hb_ref[...], win_ref[:, group * GROUP_W:(group + 1) * GROUP_W])

    def rotary(t):
        parts = []
        for hd in range(N_HEADS):
            th = t[:, _head_cols(hd)]
            parts.append(th * cos_ref[...] + pltpu.roll(th, HEAD_DIM // 2, 1) * sin_ref[...])
        return jnp.concatenate(parts, axis=1)

    def project(group):
        acc = proj(group)
        if group == 0:
            rq_ref[...] = rotary(acc).astype(BF16)
        elif group == 1:
            rk = rotary(acc) * (HEAD_DIM ** -0.5)
            for hd in range(N_HEADS):
                rkt = rk[:, _head_cols(hd)].T
                rk_ref[hd] = rkt.astype(BF16)
                rkdec_ref[hd] = (rkt * rkd_ref[hd]).astype(BF16)
        elif group == 2:
            rv_ref[...] = acc.astype(BF16)
        elif group == 3:
            rg_ref[...] = _silu(acc)
        elif group == 4:
            hq_ref[...] = _silu(acc)
        elif group == 5:
            z = acc
            lb = lb_ref[...]
            e = jnp.exp(-jnp.abs(z))
            r = 1.0 / (1.0 + e)
            er = e * r
            a2 = jnp.log2(lb + (1.0 - lb) * jnp.where(z >= 0, r, er))
            hk_ref[...] = (1.0 - lb) * jnp.where(z >= 0, er, r)
            for hd in range(N_HEADS):
                for c in range(HG_CHUNKS):
                    a_ref[hd // HEADS_PER_SCAN, pl.ds(_scan_base(hd, c), HG_CHUNK), :] = (
                        a2[c * HG_CHUNK:(c + 1) * HG_CHUNK, _head_cols(hd)])
        elif group == 6:
            hv = acc
            hv_ref[...] = hv
            hvb_ref[...] = hv.astype(BF16)
        else:
            hg_ref[...] = _silu(acc)

    def scan():
        for p in range(N_HEADS // HEADS_PER_SCAN):
            run = jnp.zeros((SUBLANES, HEAD_DIM), F32)
            for t in range(HG_CHUNK):
                rows = pl.ds(t, SUBLANES, stride=SCAN_PITCH)
                run = run + a_ref[p, rows, :]
                b_ref[p, rows, :] = run

    def hg_factors(c, hd):
        cols = _head_cols(hd)
        p = hd // HEADS_PER_SCAN
        base = _scan_base(hd, c)
        b_row = lambda r: b_ref[p, base + r:base + r + 1, :]
        for r0 in range(0, HG_CHUNK, FACTOR_ROWS):
            rows = slice(c * HG_CHUNK + r0, c * HG_CHUNK + r0 + FACTOR_ROWS)
            q = hq_ref[rows, cols]
            k = hk_ref[rows, cols]
            a = a_ref[p, base + r0:base + r0 + FACTOR_ROWS, :]
            b = b_ref[p, base + r0:base + r0 + FACTOR_ROWS, :]
            factors = _hg_level_factors(q, k, a, b, b_row, r0)
            factors.append((q * jnp.exp2(b)).astype(BF16))
            factors.append((k * jnp.exp2(b_row(HG_CHUNK - 1) - b)).astype(BF16))
            for i, f in enumerate(factors):
                xf_ref[i, rows, cols] = f
        rows = slice(c * HG_CHUNK, (c + 1) * HG_CHUNK)
        for i, src in enumerate(list(range(len(HG_LEVELS))) + [HG_FACTORS - 1]):
            xt_ref[i, c * N_HEADS + hd] = xf_ref[src, rows, cols].T

    def hg_scores(c):
        for hd in range(N_HEADS):
            rows = slice(c * HG_CHUNK, (c + 1) * HG_CHUNK)
            cols = _head_cols(hd)
            s = jnp.zeros((HG_CHUNK, HG_CHUNK), BF16)
            for li in range(len(HG_LEVELS)):
                xl = xf_ref[li, rows, cols]
                s = s + _dot(xl, xt_ref[li, c * N_HEADS + hd]).astype(BF16) * lmask_ref[li]
            hs_ref[c * N_HEADS + hd] = s

    def hg_matmuls(c):
        out = []
        for hd in range(N_HEADS):
            rows = slice(c * HG_CHUNK, (c + 1) * HG_CHUNK)
            cols = _head_cols(hd)
            vb = hvb_ref[rows, cols]
            intra = _dot(hs_ref[c * N_HEADS + hd], vb)
            inter = _dot(xf_ref[HG_FACTORS - 2, rows, cols], hstate_ref[hd].astype(BF16))
            update = _dot(xt_ref[len(HG_LEVELS), c * N_HEADS + hd], vb)
            out.append((intra, inter, update))
        return out

    def hg_finish(c, results):
        for hd, (intra, inter, update) in enumerate(results):
            rows = slice(c * HG_CHUNK, (c + 1) * HG_CHUNK)
            cols = _head_cols(hd)
            last = _scan_base(hd, c) + HG_CHUNK - 1
            b_last = b_ref[hd // HEADS_PER_SCAN, last:last + 1, :]
            diag = jnp.sum(hq_ref[rows, cols] * hk_ref[rows, cols], axis=-1, keepdims=True)
            o = intra + diag * hv_ref[rows, cols] + inter
            decay = jnp.exp2(jnp.broadcast_to(b_last, (HEAD_DIM, HEAD_DIM)).T)
            hstate_ref[hd] = hstate_ref[hd] * decay + update
            merged_ref[rows, GROUP_W + hd * HEAD_DIM:GROUP_W + (hd + 1) * HEAD_DIM] = (
                _head_norm(o, hnw_ref[:, cols]) * hg_ref[rows, cols]).astype(BF16)

    def ret_scores(c):
        for hd in range(N_HEADS):
            rows = slice(c * RET_CHUNK, (c + 1) * RET_CHUNK)
            cols = _head_cols(hd)
            rs_ref[c * N_HEADS + hd] = (_dot(rq_ref[rows, cols], rk_ref[hd, :, rows]) * rdm_ref[hd]).astype(BF16)

    def ret_matmuls(c):
        out = []
        for hd in range(N_HEADS):
            rows = slice(c * RET_CHUNK, (c + 1) * RET_CHUNK)
            cols = _head_cols(hd)
            vb = rv_ref[rows, cols]
            intra = _dot(rs_ref[c * N_HEADS + hd], vb)
            inter = _dot(rq_ref[rows, cols], rstate_ref[hd].astype(BF16))
            update = _dot(rkdec_ref[hd, :, rows], vb)
            out.append((intra, inter, update))
        return out

    def ret_finish(c, results):
        for hd, (intra, inter, update) in enumerate(results):
            rows = slice(c * RET_CHUNK, (c + 1) * RET_CHUNK)
            cols = _head_cols(hd)
            o = intra + inter * rqd_ref[rows, cols]
            rstate_ref[hd] = rstate_ref[hd] * rcd_ref[hd] + update
            merged_ref[rows, cols] = (_head_norm(o, rnw_ref[:, cols]) * rg_ref[rows, cols]).astype(BF16)

    hb_ref[...] = _rms_mod(x_ref[...], n1_ref[...], mod_ref[1:2, :], mod_ref[0:1, :]).astype(BF16)

    chunk_heads = [(c, hd) for c in range(HG_CHUNKS) for hd in range(N_HEADS)]
    project(GROUP_ORDER[0])
    project(GROUP_ORDER[1])
    scan()
    rest = GROUP_ORDER[2:]
    per_group = -(-len(chunk_heads) // len(rest))
    for i, group in enumerate(rest):
        project(group)
        for c, hd in chunk_heads[i * per_group:(i + 1) * per_group]:
            hg_factors(c, hd)

    ret_per_hg = RET_CHUNK // HG_CHUNK

    def scores(c):
        if c < HG_CHUNKS:
            if c % ret_per_hg == 0:
                ret_scores(c // ret_per_hg)
            hg_scores(c)

    scores(0)
    scores(1)
    for c in range(HG_CHUNKS):
        last_of_ret = (c + 1) % ret_per_hg == 0
        r = ret_matmuls(c // ret_per_hg) if last_of_ret else None
        h = hg_matmuls(c)
        if last_of_ret:
            ret_finish(c // ret_per_hg, r)
        hg_finish(c, h)
        scores(c + 2)

    o_ref[...] = x_ref[...] + mod_ref[2:3, :] * _dot(merged_ref[...], wout_ref[...])


def _mixer_layer(layer, x, mod_l, n1, win, wout, rnw, hnw, lb, cos2, sin2, rdm, rqd, rkd, rcd, lmask):
    bsz, seq, d = x.shape
    ts = SEQ_TILE
    grid = (bsz, seq // ts)
    x_spec = pl.BlockSpec((None, ts, d), lambda b, t: (b, t, 0))
    tab_spec = pl.BlockSpec((ts, HEAD_DIM), lambda b, t: (t, 0))
    group = lambda dtype: pltpu.VMEM((ts, GROUP_W), dtype)
    scan = pltpu.VMEM((N_HEADS // HEADS_PER_SCAN, SUBLANES * SCAN_PITCH, HEAD_DIM), F32)
    keys_t = pltpu.VMEM((N_HEADS, HEAD_DIM, ts), BF16)
    return pl.pallas_call(
        _mixer_kernel,
        grid=grid,
        in_specs=[
            x_spec,
            pl.BlockSpec((None, N_MOD, d), lambda b, t: (b, 0, 0)),
            _const_spec(n1.shape), _layer_spec(win, layer), _layer_spec(wout, layer),
            _const_spec(rnw.shape), _const_spec(hnw.shape), _const_spec(lb.shape),
            tab_spec, tab_spec,
            _const_spec(rdm.shape), _const_spec(rqd.shape), _const_spec(rkd.shape), _const_spec(rcd.shape),
            _const_spec(lmask.shape),
        ],
        out_specs=x_spec,
        out_shape=jax.ShapeDtypeStruct(x.shape, x.dtype),
        scratch_shapes=[
            pltpu.VMEM((ts, d), BF16),
            group(BF16), keys_t, keys_t, group(BF16), group(F32),
            group(F32), group(F32), group(F32), group(BF16), group(F32), scan, scan,
            pltpu.VMEM((HG_FACTORS, ts, GROUP_W), BF16),
            pltpu.VMEM((HG_CHUNKS * N_HEADS, HG_CHUNK, HG_CHUNK), BF16),
            pltpu.VMEM((ts // RET_CHUNK * N_HEADS, RET_CHUNK, RET_CHUNK), BF16),
            pltpu.VMEM((len(HG_LEVELS) + 1, HG_CHUNKS * N_HEADS, HEAD_DIM, HG_CHUNK), BF16),
            pltpu.VMEM((ts, 2 * GROUP_W), BF16),
            pltpu.VMEM((N_HEADS, HEAD_DIM, HEAD_DIM), F32),
            pltpu.VMEM((N_HEADS, HEAD_DIM, HEAD_DIM), F32),
        ],
        compiler_params=pltpu.CompilerParams(
            dimension_semantics=("arbitrary", "arbitrary"), vmem_limit_bytes=VMEM_LIMIT_BYTES),
        name="mixer",
    )(x, mod_l, n1, win, wout, rnw, hnw, lb, cos2, sin2, rdm, rqd, rkd, rcd, lmask)


def _ffn_kernel(*refs, final):
    if final:
        (x_ref, mod_ref, n2_ref, wg_ref, wv_ref, wd_ref, cw_ref, cb_ref, fw_ref,
         o_ref, a_ref, y_ref) = refs
    else:
        (x_ref, mod_ref, n2_ref, wg_ref, wv_ref, wd_ref, cw_ref, cb_ref,
         o_ref, a_ref, y_ref) = refs
    ts = x_ref.shape[0]

    @pl.when(pl.program_id(1) == 0)
    def _():
        a_ref[0:CARRY_ROWS, :] = jnp.zeros((CARRY_ROWS, D_FF), F32)

    @pl.when(pl.program_id(1) != 0)
    def _():
        a_ref[0:CARRY_ROWS, :] = a_ref[ts:ts + CARRY_ROWS, :]

    x = x_ref[...]
    mod = mod_ref[...]
    hb = _rms_mod(x, n2_ref[...], mod[4:5], mod[3:4]).astype(BF16)
    for j in range(D_FF // FF_BLOCK):
        cols = slice(j * FF_BLOCK, (j + 1) * FF_BLOCK)
        a_ref[CARRY_ROWS:CARRY_ROWS + ts, cols] = _dot(hb, wg_ref[:, cols])
        acc = cb_ref[:, cols] + a_ref[CARRY_ROWS - 2:CARRY_ROWS - 2 + ts, cols] * cw_ref[0:1, cols]
        acc = acc + a_ref[CARRY_ROWS - 1:CARRY_ROWS - 1 + ts, cols] * cw_ref[1:2, cols]
        acc = acc + a_ref[CARRY_ROWS:CARRY_ROWS + ts, cols] * cw_ref[2:3, cols]
        y_ref[:, cols] = (_silu(acc) * _dot(hb, wv_ref[:, cols])).astype(BF16)
    out = x + mod[5:6] * _dot(y_ref[...], wd_ref[...])
    if final:
        out = out * lax.rsqrt(jnp.mean(out * out, axis=-1, keepdims=True) + EPS) * fw_ref[...]
    o_ref[...] = out


def _ffn_layer(layer, x, mod_l, n2, wg, wv, wd, cw, cb, final_w=None):
    bsz, seq, d = x.shape
    ts = FFN_TILE
    final = final_w is not None
    x_spec = pl.BlockSpec((None, ts, d), lambda b, t: (b, t, 0))
    args = [x, mod_l, n2, wg, wv, wd, cw, cb] + ([final_w] if final else [])
    in_specs = [x_spec, pl.BlockSpec((None, N_MOD, d), lambda b, t: (b, 0, 0)), _const_spec(n2.shape)]
    in_specs += [_layer_spec(w, layer) for w in (wg, wv, wd)]
    in_specs += [_const_spec(a.shape) for a in args[6:]]
    return pl.pallas_call(
        functools.partial(_ffn_kernel, final=final),
        grid=(bsz, seq // ts),
        in_specs=in_specs,
        out_specs=x_spec,
        out_shape=jax.ShapeDtypeStruct(x.shape, x.dtype),
        scratch_shapes=[
            pltpu.VMEM((ts + 2 * CARRY_ROWS, D_FF), F32),
            pltpu.VMEM((ts, D_FF), BF16),
        ],
        compiler_params=pltpu.CompilerParams(
            dimension_semantics=("arbitrary", "arbitrary"), vmem_limit_bytes=VMEM_LIMIT_BYTES),
        name="ffn",
    )(*args)


def _retention_tables():
    log_gamma = jnp.log(1.0 - jnp.exp2(-5.0 - jnp.arange(N_HEADS, dtype=F32)))
    lg = log_gamma[:, None, None]
    pos = jnp.arange(RET_CHUNK, dtype=F32)
    rel = pos[:, None] - pos[None, :]
    decay_mask = jnp.where(rel[None] >= 0, jnp.exp(lg * jnp.maximum(rel, 0.0)[None]), 0.0)
    ones = jnp.ones((1, 1, HEAD_DIM), F32)
    q_decay = jnp.exp(lg * (pos + 1.0)[None, :, None]) * ones
    k_decay = jnp.exp(lg * (RET_CHUNK - 1.0 - pos)[None, :, None]) * ones
    chunk_decay = jnp.exp(lg * RET_CHUNK) * jnp.ones((1, HEAD_DIM, HEAD_DIM), F32)

    def tile_layout(t):
        t = jnp.transpose(t, (1, 0, 2)).reshape(RET_CHUNK, GROUP_W)
        return jnp.tile(t, (SEQ_TILE // RET_CHUNK, 1))

    k_decay_t = jnp.tile(k_decay[:, :, 0], (1, SEQ_TILE // RET_CHUNK))[:, None, :]
    return decay_mask, tile_layout(q_decay), k_decay_t, chunk_decay


def kernel(x, c, w_in, w_out, ret_norm_w, hgrn_norm_w, hgrn_lb_logits, norm1_w, norm2_w,
           ada_w, ada_b, w_gate, w_val, conv_w, conv_b, w_down, final_norm_w):
    depth = w_in.shape[0]
    seq = x.shape[1]
    assert x.shape[2] == D_MODEL and seq % SEQ_TILE == 0 and SEQ_TILE % RET_CHUNK == 0 and seq % FFN_TILE == 0
    assert HEADS_PER_SCAN * HG_CHUNKS == SUBLANES and N_HEADS % HEADS_PER_SCAN == 0

    mods = _ada_mods(c, ada_w, ada_b)
    cos2, sin2 = _rope_tables(seq)
    rdm, rqd, rkd, rcd = _retention_tables()
    lmask = jnp.asarray(_hg_level_masks(), BF16)
    p = jax.nn.softmax(hgrn_lb_logits.astype(F32), axis=0)
    lower_bounds = jnp.cumsum(p, axis=0) - p[0:1]

    win, wout, wg, wv, wd = (w.astype(BF16) for w in (w_in, w_out, w_gate, w_val, w_down))
    row = lambda v: v.reshape(1, -1)
    for l in range(depth):
        x = _mixer_layer(l, x, mods[l], row(norm1_w[l]), win, wout,
                         row(ret_norm_w[l]), row(hgrn_norm_w[l]), row(lower_bounds[l]),
                         cos2, sin2, rdm, rqd, rkd, rcd, lmask)
        x = _ffn_layer(l, x, mods[l], row(norm2_w[l]), wg, wv, wd, conv_w[l], row(conv_b[l]),
                       final_w=row(final_norm_w) if l == depth - 1 else None)
    return x
```
